```python
import math, functools
import jax, jax.numpy as jnp
from jax import lax
import numpy as np

D_MODEL = 1024
BATCH = 4
SEQ = 4096
DEPTH = 2
DEC_BATCH = 32
DEC_SEQ = 4
PAST_LEN = 8192
PAGE_SIZE = 128

N_META = 16
CHUNK = 128
HA = 4
DKA = 128
DVA = 128
CONV_W = 4
HB = 4
DHB = 128
HC = 4
DKC = 128
DVC = 128
ROPE_BASE = 10000.0
N_BRANCH = 3
BRANCH_W = 512
N_GROUPS = 4
EXP_PER_GROUP = 4
TOP_K = 2
D_FF_EXPERT = 512
EPS = 1e-6
NEG = -1e30

A_QKV = 2 * HA * DKA + HA * DVA
SPLIT_WIDTHS = (A_QKV, HA * DVA, HA, HA, HB * DHB, HB * DHB, HB * DHB, HB,
                HC * DKC, HC * DKC, HC * DVC, HC * DVC, N_BRANCH * D_MODEL)
N_IN = sum(SPLIT_WIDTHS)

kernel_name = 'hybrid_deltanet_fox_retention_hmoe_step'


def rmsnorm(x, g):
    xf = x.astype(jnp.float32)
    y = xf * lax.rsqrt(jnp.mean(xf * xf, axis=-1, keepdims=True) + EPS)
    return (y * g.astype(jnp.float32)).astype(x.dtype)


def l2norm(x):
    return x * lax.rsqrt(jnp.sum(x * x, axis=-1, keepdims=True) + EPS)


def causal_conv(buf, u, w):
    full = jnp.concatenate([buf, u], axis=1)
    t = u.shape[1]
    out = full[:, 0:t] * w[0]
    for i in range(1, CONV_W):
        out = out + full[:, i:i + t] * w[i]
    return jax.nn.silu(out.astype(jnp.float32)), full[:, -(CONV_W - 1):]


def rotary(x, pos):
    half = x.shape[-1] // 2
    inv = ROPE_BASE ** (-jnp.arange(half, dtype=jnp.float32) / half)
    ang = pos.astype(jnp.float32)[:, None] * inv[None, :]
    cos = jnp.cos(ang)[None, :, None, :]
    sin = jnp.sin(ang)[None, :, None, :]
    x1, x2 = x[..., :half], x[..., half:]
    return jnp.concatenate([x1 * cos - x2 * sin, x1 * sin + x2 * cos], axis=-1)


def delta_chunk(s, q, k, v, g, beta):
    t = q.shape[1]
    gc = jnp.cumsum(g, axis=1).transpose(0, 2, 1)
    incl = jnp.tril(jnp.ones((t, t), bool))
    strict = jnp.tril(jnp.ones((t, t), bool), -1)
    diff = gc[..., :, None] - gc[..., None, :]
    decay = jnp.where(incl, jnp.exp(jnp.where(incl, diff, 0.0)), 0.0)
    qh, kh, vh = (a.transpose(0, 2, 1, 3) for a in (q, k, v))
    bh = beta.transpose(0, 2, 1)[..., None]
    kb = kh * bh
    m = jnp.where(strict, jnp.einsum('nhid,nhjd->nhij', kb, kh) * decay, 0.0)
    eye = jnp.broadcast_to(jnp.eye(t, dtype=m.dtype), m.shape)
    w_inv = lax.linalg.triangular_solve(m + eye, eye, left_side=True, lower=True, unit_diagonal=True)
    u = w_inv @ (vh * bh)
    w = w_inv @ (kb * jnp.exp(gc)[..., None])
    v_new = u - w @ s
    attn = jnp.einsum('nhid,nhjd->nhij', qh, kh) * decay
    o = (qh * jnp.exp(gc)[..., None]) @ s + attn @ v_new
    g_last = gc[..., -1:]
    s_new = s * jnp.exp(g_last)[..., None] + jnp.einsum(
        'nhtk,nhtv->nhkv', kh * jnp.exp(g_last - gc)[..., None], v_new)
    return s_new, o.transpose(0, 2, 1, 3)


def retention_chunk(s, q, k, v, log_gamma):
    t = q.shape[1]
    idx = jnp.arange(t, dtype=jnp.float32)
    incl = jnp.tril(jnp.ones((t, t), bool))
    rel = jnp.where(incl, idx[:, None] - idx[None, :], 0.0)
    dmat = jnp.where(incl, jnp.exp(rel[None] * log_gamma[:, None, None]), 0.0)
    qh, kh, vh = (a.transpose(0, 2, 1, 3) for a in (q, k, v))
    inner = (jnp.einsum('nhid,nhjd->nhij', qh, kh) * dmat) @ vh
    cross = (qh * jnp.exp((idx + 1.0)[None, :] * log_gamma[:, None])[None, :, :, None]) @ s
    kdec = kh * jnp.exp((t - 1.0 - idx)[None, :] * log_gamma[:, None])[None, :, :, None]
    s_new = s * jnp.exp(t * log_gamma)[None, :, None, None] + jnp.einsum('nhtk,nhtv->nhkv', kdec, vh)
    return s_new, (inner + cross).transpose(0, 2, 1, 3)


def chunked_scan(step, s, arrays):
    s, o_head = step(s, *(a[:, :N_META] for a in arrays))
    n, l = arrays[0].shape[:2]
    nb = (l - N_META) // CHUNK
    xs = tuple(jnp.swapaxes(a[:, N_META:].reshape((n, nb, CHUNK) + a.shape[2:]), 0, 1) for a in arrays)
    s, o = lax.scan(lambda c, xc: step(c, *xc), s, xs)
    o = jnp.swapaxes(o, 0, 1).reshape((n, nb * CHUNK) + o.shape[3:])
    return s, jnp.concatenate([o_head, o], axis=1)


def fox_block(qb, cq, pq, k, v, ck, pk):
    s = jnp.einsum('nqhd,nkhd->nhqk', qb, k) * DHB ** -0.5 + cq[..., :, None] - ck[..., None, :]
    s = jnp.where(pk[None, :] <= pq[:, None], s, NEG)
    p = jax.nn.softmax(s, axis=-1)
    return jnp.einsum('nhqk,nkhd->nqhd', p, v)


def fox_prompt(q, k, v, logf):
    n, l = q.shape[:2]
    c = jnp.cumsum(logf, axis=1).transpose(0, 2, 1)
    pos = jnp.arange(l)
    o_head = fox_block(q[:, :N_META], c[..., :N_META], pos[:N_META], k, v, c, pos)
    nb = (l - N_META) // CHUNK
    qs = jnp.swapaxes(q[:, N_META:].reshape(n, nb, CHUNK, HB, DHB), 0, 1)
    cs = c[..., N_META:].reshape(n, HB, nb, CHUNK).transpose(2, 0, 1, 3)
    ps = pos[N_META:].reshape(nb, CHUNK)
    o = lax.map(lambda a: fox_block(a[0], a[1], a[2], k, v, c, pos), (qs, cs, ps))
    o = jnp.swapaxes(o, 0, 1).reshape(n, nb * CHUNK, HB, DHB)
    return jnp.concatenate([o_head, o], axis=1)


def fox_sample(q, k, v, logf, k_past, v_past, logf_past):
    t = q.shape[1]
    p_len = k_past.shape[1]
    c_past = jnp.cumsum(logf_past, axis=1)
    c_new = c_past[:, -1:] + jnp.cumsum(logf, axis=1)
    cq = c_new.transpose(0, 2, 1)[..., :, None]
    scale = DHB ** -0.5
    s_past = jnp.einsum('nqhd,nkhd->nhqk', q, k_past) * scale + cq - c_past.transpose(0, 2, 1)[..., None, :]
    s_new = jnp.einsum('nqhd,nkhd->nhqk', q, k) * scale + cq - c_new.transpose(0, 2, 1)[..., None, :]
    s_new = jnp.where(jnp.tril(jnp.ones((t, t), bool)), s_new, NEG)
    p = jax.nn.softmax(jnp.concatenate([s_past, s_new], axis=-1), axis=-1)
    return (jnp.einsum('nhqk,nkhd->nqhd', p[..., :p_len], v_past)
            + jnp.einsum('nhqk,nkhd->nqhd', p[..., p_len:], v))


def token_mixers(xn, pos, delta_s, conv_buf, ret_s, kv_past,
                 w_in, conv_w, a_log, dt_bias, a_norm, b_fbias, c_norm, w_branch, w_out):
    f32 = jnp.float32
    dt = xn.dtype
    n, t = xn.shape[:2]
    prompt = kv_past is None
    run = chunked_scan if prompt else (lambda step, s, arrs: step(s, *arrs))
    pts = [int(i) for i in np.cumsum(SPLIT_WIDTHS)[:-1]]
    (a_qkv, a_z, a_beta, a_alpha, b_q, b_k, b_v, b_f,
     c_q, c_k, c_v, c_g, gates) = jnp.split(xn @ w_in, pts, axis=-1)

    def heads(a, h):
        return a.astype(f32).reshape(n, t, h, -1)

    conv_out, conv_new = causal_conv(conv_buf.astype(dt), a_qkv, conv_w.astype(dt))
    qa, ka, va = jnp.split(conv_out, [HA * DKA, 2 * HA * DKA], axis=-1)
    qa = l2norm(heads(qa, HA)) * DKA ** -0.5
    ka = l2norm(heads(ka, HA))
    va = heads(va, HA)
    beta = jax.nn.sigmoid(a_beta.astype(f32))
    g = -jnp.exp(a_log.astype(f32)) * jax.nn.softplus(a_alpha.astype(f32) + dt_bias.astype(f32))
    sa_new, oa = run(delta_chunk, delta_s.astype(f32), (qa, ka, va, g, beta))
    oa = rmsnorm(oa, a_norm) * jax.nn.silu(heads(a_z, HA))

    qb, kb, vb = heads(b_q, HB), heads(b_k, HB), heads(b_v, HB)
    logf = jax.nn.log_sigmoid(b_f.astype(f32) + b_fbias.astype(f32))
    if prompt:
        ob = fox_prompt(qb, kb, vb, logf)
    else:
        ob = fox_sample(qb, kb, vb, logf, *kv_past)

    qc = rotary(heads(c_q, HC), pos)
    kc = rotary(heads(c_k, HC), pos) * DKC ** -0.5
    vc = heads(c_v, HC)
    log_gamma = jnp.log(1.0 - jnp.exp2(-5.0 - jnp.arange(HC, dtype=f32)))
    sc_new, oc = run(functools.partial(retention_chunk, log_gamma=log_gamma), ret_s.astype(f32), (qc, kc, vc))
    oc = rmsnorm(oc, c_norm) * jax.nn.silu(heads(c_g, HC))

    br = jnp.stack([oa.reshape(n, t, BRANCH_W), ob.reshape(n, t, BRANCH_W),
                    oc.reshape(n, t, BRANCH_W)], axis=0).astype(dt)
    proj = jnp.einsum('bntw,bwd->bntd', br, w_branch)
    gate = jax.nn.sigmoid(gates.astype(f32).reshape(n, t, N_BRANCH, D_MODEL))
    merged = jnp.einsum('bntd,ntbd->ntd', proj.astype(f32), gate).astype(dt)
    return merged @ w_out, (kb, vb, logf, sa_new, conv_new, sc_new)


def moe(x, w_rg, b_rg, w_re, b_re, w1, w3, w2):
    f32 = jnp.float32
    shp = x.shape
    xt = x.reshape(-1, shp[-1])
    m = xt.shape[0]
    lg_g = (xt @ w_rg).astype(f32) + b_rg.astype(f32)
    p_g = jax.nn.softmax(lg_g, axis=-1)
    g_star = jnp.argmax(lg_g, axis=-1)
    lg_e = (xt @ w_re).astype(f32).reshape(m, N_GROUPS, EXP_PER_GROUP) + b_re.astype(f32)
    lg_sel = jnp.take_along_axis(lg_e, g_star[:, None, None], axis=1)[:, 0]
    top_v, top_i = lax.top_k(lg_sel, TOP_K)
    w_top = jax.nn.softmax(top_v, axis=-1) * jnp.take_along_axis(p_g, g_star[:, None], axis=1)
    gate_e = jnp.einsum('mk,mke->me', w_top, jax.nn.one_hot(top_i, EXP_PER_GROUP, dtype=f32))
    gate = (jax.nn.one_hot(g_star, N_GROUPS, dtype=f32)[:, :, None] * gate_e[:, None, :]).astype(x.dtype)
    y = jnp.zeros_like(xt)
    for gi in range(N_GROUPS):
        h = jax.nn.silu(jnp.einsum('md,edf->mef', xt, w1[gi])) * jnp.einsum('md,edf->mef', xt, w3[gi])
        y = y + jnp.einsum('mef,efd->md', h * gate[:, gi, :, None], w2[gi])
    return y.reshape(shp)


def gather_pages(pool, page_table):
    pages = pool[page_table]
    n, n_pages = page_table.shape
    return pages.reshape((n, n_pages * pool.shape[1]) + pool.shape[2:]).astype(jnp.float32)


def setup_inputs(seed: int = 0) -> dict:
    key = jax.random.key(seed)
    ks = jax.random.split(key, 32)
    f32 = jnp.float32
    n_pages = PAST_LEN // PAGE_SIZE
    n_pool = (DEC_BATCH * n_pages * 5) // 4

    def nrm(k, shape, scale):
        return jax.random.normal(k, shape, f32) * scale

    dt0 = jnp.exp(jax.random.uniform(ks[16], (DEPTH, HA), f32, math.log(1e-3), math.log(1e-1)))
    return {
        'x_prompt': nrm(ks[0], (BATCH, SEQ, D_MODEL), 1.0),
        'x_sample': nrm(ks[1], (DEC_BATCH, DEC_SEQ, D_MODEL), 1.0),
        'cache_k': nrm(ks[2], (DEPTH, n_pool, PAGE_SIZE, HB, DHB), 1.0),
        'cache_v': nrm(ks[3], (DEPTH, n_pool, PAGE_SIZE, HB, DHB), 1.0),
        'cache_logf': jax.nn.log_sigmoid(3.0 + nrm(ks[4], (DEPTH, n_pool, PAGE_SIZE, HB), 1.0)),
        'page_table': jax.random.permutation(ks[5], n_pool)[:DEC_BATCH * n_pages]
                      .reshape(DEC_BATCH, n_pages).astype(jnp.int32),
        'state_delta': nrm(ks[6], (DEPTH, DEC_BATCH, HA, DKA, DVA), 0.1),
        'state_conv': nrm(ks[7], (DEPTH, DEC_BATCH, CONV_W - 1, A_QKV), 1.0),
        'state_ret': nrm(ks[8], (DEPTH, DEC_BATCH, HC, DKC, DVC), 0.5),
        'meta_tokens': nrm(ks[9], (N_META, D_MODEL), 1.0),
        'norm_mix': 1.0 + nrm(ks[10], (DEPTH, D_MODEL), 0.02),
        'norm_ffn': 1.0 + nrm(ks[11], (DEPTH, D_MODEL), 0.02),
        'norm_final': 1.0 + nrm(ks[12], (D_MODEL,), 0.02),
        'w_in': nrm(ks[13], (DEPTH, D_MODEL, N_IN), D_MODEL ** -0.5),
        'conv_w': nrm(ks[14], (DEPTH, CONV_W, A_QKV), CONV_W ** -0.5),
        'a_log': jnp.log(jax.random.uniform(ks[15], (DEPTH, HA), f32, 1.0, 16.0)),
        'dt_bias': dt0 + jnp.log(-jnp.expm1(-dt0)),
        'a_norm': 1.0 + nrm(ks[17], (DEPTH, DVA), 0.02),
        'b_fbias': 2.0 + nrm(ks[18], (DEPTH, HB), 0.5),
        'c_norm': 1.0 + nrm(ks[19], (DEPTH, DVC), 0.02),
        'w_branch': nrm(ks[20], (DEPTH, N_BRANCH, BRANCH_W, D_MODEL), BRANCH_W ** -0.5),
        'w_out': nrm(ks[21], (DEPTH, D_MODEL, D_MODEL), D_MODEL ** -0.5),
        'w_router_group': nrm(ks[22], (DEPTH, D_MODEL, N_GROUPS), D_MODEL ** -0.5),
        'b_router_group': nrm(ks[23], (DEPTH, N_GROUPS), 0.01),
        'w_router_expert': nrm(ks[24], (DEPTH, D_MODEL, N_GROUPS * EXP_PER_GROUP), D_MODEL ** -0.5),
        'b_router_expert': nrm(ks[25], (DEPTH, N_GROUPS, EXP_PER_GROUP), 0.01),
        'w1': nrm(ks[26], (DEPTH, N_GROUPS, EXP_PER_GROUP, D_MODEL, D_FF_EXPERT), D_MODEL ** -0.5),
        'w3': nrm(ks[27], (DEPTH, N_GROUPS, EXP_PER_GROUP, D_MODEL, D_FF_EXPERT), D_MODEL ** -0.5),
        'w2': nrm(ks[28], (DEPTH, N_GROUPS, EXP_PER_GROUP, D_FF_EXPERT, D_MODEL), D_FF_EXPERT ** -0.5),
    }


def reference(x_prompt, x_sample, cache_k, cache_v, cache_logf, page_table, state_delta, state_conv, state_ret,
              meta_tokens, norm_mix, norm_ffn, norm_final, w_in, conv_w, a_log, dt_bias, a_norm, b_fbias, c_norm,
              w_branch, w_out, w_router_group, b_router_group, w_router_expert, b_router_expert, w1, w3, w2):
    f32 = jnp.float32
    b = x_prompt.shape[0]
    xp = jnp.concatenate([jnp.broadcast_to(meta_tokens.astype(x_prompt.dtype)[None], (b, N_META, D_MODEL)),
                          x_prompt], axis=1)
    pos_p = jnp.arange(xp.shape[1])
    xs = x_sample
    ns, ts = xs.shape[:2]
    past = page_table.shape[1] * cache_k.shape[2]
    pos_s = past + jnp.arange(ts)
    zero_delta = jnp.zeros((b, HA, DKA, DVA), f32)
    zero_conv = jnp.zeros((b, CONV_W - 1, A_QKV), xp.dtype)
    zero_ret = jnp.zeros((b, HC, DKC, DVC), f32)
    new_p = [[] for _ in range(6)]
    new_s = [[] for _ in range(6)]
    for l in range(DEPTH):
        lw = (w_in[l], conv_w[l], a_log[l], dt_bias[l], a_norm[l], b_fbias[l], c_norm[l], w_branch[l], w_out[l])
        fw = (w_router_group[l], b_router_group[l], w_router_expert[l], b_router_expert[l], w1[l], w3[l], w2[l])
        mo, st = token_mixers(rmsnorm(xp, norm_mix[l]), pos_p, zero_delta, zero_conv, zero_ret, None, *lw)
        xp = xp + mo
        xp = xp + moe(rmsnorm(xp, norm_ffn[l]), *fw)
        for lst, val in zip(new_p, st):
            lst.append(val)
        kv_past = (gather_pages(cache_k[l], page_table), gather_pages(cache_v[l], page_table),
                   gather_pages(cache_logf[l], page_table))
        mo, st = token_mixers(rmsnorm(xs, norm_mix[l]), pos_s, state_delta[l], state_conv[l], state_ret[l],
                              kv_past, *lw)
        xs = xs + mo
        xs = xs + moe(rmsnorm(xs, norm_ffn[l]), *fw)
        for lst, val in zip(new_s, st):
            lst.append(val)
    y_prompt = rmsnorm(xp, norm_final)[:, N_META:]
    y_sample = rmsnorm(xs, norm_final)
    k_p, v_p, lf_p, sd_p, sc_p, sr_p = (jnp.stack(a) for a in new_p)
    k_s, v_s, lf_s, sd_s, sc_s, sr_s = (jnp.stack(a) for a in new_s)
    return (y_prompt, y_sample, k_p, v_p, lf_p, sd_p, sc_p, sr_p, k_s, v_s, lf_s, sd_s, sc_s, sr_s)
```

```python
import functools

import jax
import jax.numpy as jnp
import numpy as np
from jax import lax
from jax.experimental import pallas as pl
from jax.experimental.pallas import tpu as pltpu

F32 = jnp.float32
BF16 = jnp.bfloat16

D_MODEL = 1024
N_META = 16
CHUNK = 128
PAD_FRONT = CHUNK - N_META
N_HEADS = 4
D_HEAD = 128
BRANCH_W = N_HEADS * D_HEAD
A_QKV = 3 * BRANCH_W
CONV_W = 4
N_BRANCH = 3
N_GROUPS = 4
EXP_PER_GROUP = 4
N_EXPERTS = N_GROUPS * EXP_PER_GROUP
D_FF = 512
ROPE_BASE = 10000.0
EPS = 1e-6
NEG = -1e30
SAMPLE_ROWS = 8
LANES = 128

COL_GATES = 0
COL_AQKV = 6
COL_AZ = 9
COL_BQ, COL_BK, COL_BV = 10, 11, 12
COL_CQ, COL_CK, COL_CV, COL_CG = 13, 14, 15, 16
N_MAIN = 17 * BRANCH_W
LANE_BETA, LANE_ALPHA, LANE_F = 0, 4, 8
LANE_GROUP, LANE_EXPERT = 0, 4

VMEM_LIMIT = 56 * 1024 * 1024

NN = (((1,), (0,)), ((), ()))
NT = (((1,), (1,)), ((), ()))
TN = (((0,), (0,)), ((), ()))


def _split2(x):
    hi = x.astype(BF16)
    return hi, (x - hi.astype(F32)).astype(BF16)


def _dot(a, b, dims=NN, precise=False):
    def dg(x, y):
        return lax.dot_general(x, y, dims, preferred_element_type=F32)

    if not precise:
        a = a[0] if isinstance(a, tuple) else a.astype(BF16)
        b = b[0] if isinstance(b, tuple) else b.astype(BF16)
        return dg(a, b)
    a_hi, a_lo = a if isinstance(a, tuple) else _split2(a)
    b_hi, b_lo = b if isinstance(b, tuple) else _split2(b)
    return dg(a_hi, b_hi) + dg(a_hi, b_lo) + dg(a_lo, b_hi)


def _split3(x):
    x1 = x.astype(BF16)
    r1 = x - x1.astype(F32)
    x2 = r1.astype(BF16)
    x3 = (r1 - x2.astype(F32)).astype(BF16)
    return x1, x2, x3


def _sigmoid(x):
    return 1.0 / (1.0 + jnp.exp(-x))


def _silu(x):
    return x * _sigmoid(x)


def _softplus(x):
    return jnp.maximum(x, 0.0) + jnp.log1p(jnp.exp(-jnp.abs(x)))


def _rms(x, g):
    return x * lax.rsqrt(jnp.mean(x * x, axis=-1, keepdims=True) + EPS) * g


def _params(n_axes):
    return pltpu.CompilerParams(dimension_semantics=("arbitrary",) * n_axes, vmem_limit_bytes=VMEM_LIMIT)


def _proj_kernel(x_ref, g_ref, w_ref, ws_ref, p_ref, ps_ref, xn_ref, *, precise):
    @pl.when(pl.program_id(1) == 0)
    def _():
        xn = _split2(_rms(x_ref[...], g_ref[...]))
        xn_ref[0] = xn[0]
        if precise:
            xn_ref[1] = xn[1]
        ps_ref[...] = _dot(xn, ws_ref[...], precise=precise)

    xn = (xn_ref[0], xn_ref[1]) if precise else xn_ref[0]
    p_ref[...] = _dot(xn, w_ref[...], precise=precise)


def _proj(x, g, w_main, w_small, tm, tn, precise):
    r = x.shape[0]
    return pl.pallas_call(
        functools.partial(_proj_kernel, precise=precise),
        grid=(r // tm, N_MAIN // tn),
        in_specs=[pl.BlockSpec((tm, D_MODEL), lambda i, j: (i, 0)),
                  pl.BlockSpec((1, D_MODEL), lambda i, j: (0, 0)),
                  pl.BlockSpec((D_MODEL, tn), lambda i, j: (0, j)),
                  pl.BlockSpec((D_MODEL, LANES), lambda i, j: (0, 0))],
        out_specs=[pl.BlockSpec((tm, tn), lambda i, j: (i, j)),
                   pl.BlockSpec((tm, LANES), lambda i, j: (i, 0))],
        out_shape=[jax.ShapeDtypeStruct((r, N_MAIN), F32), jax.ShapeDtypeStruct((r, LANES), F32)],
        scratch_shapes=[pltpu.VMEM((2 if precise else 1, tm, D_MODEL), BF16)],
        compiler_params=_params(2),
        name="proj_split" if precise else "proj",
    )(x, g, w_main, w_small)


def _delta_head(q, k, v, beta, gcol, grow, s, incl, strict, precise):
    mm = functools.partial(_dot, precise=precise)
    eg = jnp.exp(gcol)
    decay = jnp.where(incl, jnp.exp(jnp.where(incl, gcol - grow, 0.0)), 0.0)
    kb = k * beta
    m = jnp.where(strict, mm(kb, k, NT) * decay, 0.0)
    npow = -m
    r = npow
    for _ in range(6):
        npow = mm(npow, npow)
        r = r + npow + mm(r, npow)
    e = -(m + r + _dot(m, r, precise=True))
    r = r + e + mm(r, e)
    vb = v * beta
    kbe = kb * eg
    u = vb + mm(r, vb)
    w = kbe + mm(r, kbe)
    v_new = u - mm(w, s)
    attn = mm(q, k, NT) * decay
    o = mm(q * eg, s) + mm(attn, v_new)
    g_last = gcol[CHUNK - 1:CHUNK, :]
    s_new = s * jnp.exp(g_last) + mm(k * jnp.exp(g_last - gcol), v_new, TN)
    return o, s_new


def _seq_kernel(*refs, tc_in, valid_hi, front_pad, has_init, precise):
    if has_init:
        (aqkv_ref, az_ref, cq_ref, ck_ref, cv_ref, cg_ref, sm_ref, cos_ref, sin_ref,
         convw_ref, avec_ref, anorm_ref, cnorm_ref, dmat_ref, rtab_ref,
         sd0_ref, sr0_ref, cb0_ref,
         oa_ref, oc_ref, lf_ref, cc_ref, sd_ref, sr_ref, xbuf, cbuf, carry) = refs
    else:
        (aqkv_ref, az_ref, cq_ref, ck_ref, cv_ref, cg_ref, sm_ref, cos_ref, sin_ref,
         convw_ref, avec_ref, anorm_ref, cnorm_ref, dmat_ref, rtab_ref,
         oa_ref, oc_ref, lf_ref, cc_ref, sd_ref, sr_ref, xbuf, cbuf, carry) = refs
    mm = functools.partial(_dot, precise=precise)
    c = pl.program_id(1)

    def ext(ref, cols=None):
        v = ref[...] if cols is None else ref[:, cols]
        if tc_in == CHUNK:
            return v
        return jnp.concatenate([v, jnp.zeros((CHUNK - tc_in, v.shape[1]), v.dtype)], axis=0)

    @pl.when(c == 0)
    def _():
        carry[...] = jnp.zeros_like(carry)
        if has_init:
            xbuf[0:8, :] = cb0_ref[0]
            sd_ref[...] = sd0_ref[...]
            sr_ref[...] = sr0_ref[...]
        else:
            xbuf[0:8, :] = jnp.zeros((8, A_QKV), F32)
            sd_ref[...] = jnp.zeros_like(sd_ref)
            sr_ref[...] = jnp.zeros_like(sr_ref)

    rows = lax.broadcasted_iota(jnp.int32, (CHUNK, LANES), 0)
    lane = lax.broadcasted_iota(jnp.int32, (CHUNK, LANES), 1)
    lo = jnp.where(c == 0, front_pad, 0)
    valid = (rows >= lo) & (rows < valid_hi)
    incl = rows >= lane
    strict = rows > lane

    for j in range(A_QKV // LANES):
        cols = slice(j * LANES, (j + 1) * LANES)
        xbuf[8:8 + CHUNK, cols] = ext(aqkv_ref, cols)
        acc = xbuf[5:5 + CHUNK, cols] * convw_ref[0:1, cols]
        for i in range(1, CONV_W):
            acc = acc + xbuf[5 + i:5 + i + CHUNK, cols] * convw_ref[i:i + 1, cols]
        cbuf[:, cols] = jnp.where(valid, _silu(acc), 0.0)
        xbuf[0:8, cols] = xbuf[CHUNK:CHUNK + 8, cols]

    sm = ext(sm_ref)
    av = avec_ref[...]
    g_all = -jnp.exp(av[0:1, :]) * _softplus(sm + av[1:2, :])
    lf_all = -_softplus(-(sm + av[2:3, :]))
    is_g = (lane >= LANE_ALPHA) & (lane < LANE_ALPHA + N_HEADS)
    is_f = (lane >= LANE_F) & (lane < LANE_F + N_HEADS)
    z = jnp.where(valid & is_g, g_all, jnp.where(valid & is_f, lf_all, 0.0))
    tril = jnp.where(incl, 1.0, 0.0).astype(BF16)
    z1, z2, z3 = _split3(z)
    cum = (jnp.dot(tril, z1, preferred_element_type=F32) + jnp.dot(tril, z2, preferred_element_type=F32)
           + jnp.dot(tril, z3, preferred_element_type=F32))
    cum_t = cum.T
    cc = cum + carry[...]
    carry[...] = cc[CHUNK - 1:CHUNK, :]
    lf_ref[...] = z[:tc_in]
    cc_ref[...] = cc[:tc_in]

    beta_all = _sigmoid(sm)
    cos2 = cos_ref[...]
    sin2 = sin_ref[...]
    rtab = rtab_ref[...]
    for h in range(N_HEADS):
        hs = slice(h * D_HEAD, (h + 1) * D_HEAD)
        q = cbuf[:, hs]
        k = cbuf[:, BRANCH_W + h * D_HEAD:BRANCH_W + (h + 1) * D_HEAD]
        v = cbuf[:, 2 * BRANCH_W + h * D_HEAD:2 * BRANCH_W + (h + 1) * D_HEAD]
        q = q * lax.rsqrt(jnp.sum(q * q, axis=-1, keepdims=True) + EPS) * (D_HEAD ** -0.5)
        k = k * lax.rsqrt(jnp.sum(k * k, axis=-1, keepdims=True) + EPS)
        beta = beta_all[:, LANE_BETA + h:LANE_BETA + h + 1]
        gcol = cum[:, LANE_ALPHA + h:LANE_ALPHA + h + 1]
        grow = cum_t[LANE_ALPHA + h:LANE_ALPHA + h + 1, :]
        o, s_new = _delta_head(q, k, v, beta, gcol, grow, sd_ref[0, h], incl, strict, precise)
        sd_ref[0, h] = s_new
        oa = _rms(o, anorm_ref[...]) * _silu(ext(az_ref, hs))
        oa_ref[:, hs] = oa[:tc_in]

        qc = ext(cq_ref, hs)
        kc = ext(ck_ref, hs)
        vc = ext(cv_ref, hs)
        qc = qc * cos2 + pltpu.roll(qc, D_HEAD // 2, 1) * sin2
        kc = (kc * cos2 + pltpu.roll(kc, D_HEAD // 2, 1) * sin2) * (D_HEAD ** -0.5)
        s_r = sr_ref[0, h]
        inner = mm(mm(qc, kc, NT) * dmat_ref[h], vc)
        cross = mm(qc * rtab[:, h:h + 1], s_r)
        sr_ref[0, h] = s_r * rtab[0:1, 8 + h:9 + h] + mm(kc * rtab[:, 4 + h:5 + h], vc, TN)
        oc = _rms(inner + cross, cnorm_ref[...]) * _silu(ext(cg_ref, hs))
        oc_ref[:, hs] = oc[:tc_in]


def _seq(p, ps, cos2, sin2, convw, avec, anorm, cnorm, dmat, rtab, *, n_seq, n_chunks, tc_in,
         valid_hi, front_pad, precise, init=None):
    r = p.shape[0]

    def rows(col):
        return lambda n, c: (n * n_chunks + c, col)

    const2 = lambda n, c: (0, 0)
    state_spec = pl.BlockSpec((1, N_HEADS, D_HEAD, D_HEAD), lambda n, c: (n, 0, 0, 0))
    in_specs = [
        pl.BlockSpec((tc_in, A_QKV), rows(COL_AQKV // 3)),
        pl.BlockSpec((tc_in, BRANCH_W), rows(COL_AZ)),
        pl.BlockSpec((tc_in, BRANCH_W), rows(COL_CQ)),
        pl.BlockSpec((tc_in, BRANCH_W), rows(COL_CK)),
        pl.BlockSpec((tc_in, BRANCH_W), rows(COL_CV)),
        pl.BlockSpec((tc_in, BRANCH_W), rows(COL_CG)),
        pl.BlockSpec((tc_in, LANES), rows(0)),
        pl.BlockSpec((CHUNK, D_HEAD), lambda n, c: (c, 0)),
        pl.BlockSpec((CHUNK, D_HEAD), lambda n, c: (c, 0)),
        pl.BlockSpec((CONV_W, A_QKV), const2),
        pl.BlockSpec((8, LANES), const2),
        pl.BlockSpec((1, D_HEAD), const2),
        pl.BlockSpec((1, D_HEAD), const2),
        pl.BlockSpec((N_HEADS, CHUNK, CHUNK), lambda n, c: (0, 0, 0)),
        pl.BlockSpec((CHUNK, LANES), const2),
    ]
    args = [p, p, p, p, p, p, ps, cos2, sin2, convw, avec, anorm, cnorm, dmat, rtab]
    if init is not None:
        in_specs += [state_spec, state_spec, pl.BlockSpec((1, 8, A_QKV), lambda n, c: (n, 0, 0))]
        args += list(init)
    out_specs = [
        pl.BlockSpec((tc_in, BRANCH_W), rows(0)),
        pl.BlockSpec((tc_in, BRANCH_W), rows(0)),
        pl.BlockSpec((tc_in, LANES), rows(0)),
        pl.BlockSpec((tc_in, LANES), rows(0)),
        state_spec,
        state_spec,
    ]
    out_shape = [
        jax.ShapeDtypeStruct((r, BRANCH_W), F32),
        jax.ShapeDtypeStruct((r, BRANCH_W), F32),
        jax.ShapeDtypeStruct((r, LANES), F32),
        jax.ShapeDtypeStruct((r, LANES), F32),
        jax.ShapeDtypeStruct((n_seq, N_HEADS, D_HEAD, D_HEAD), F32),
        jax.ShapeDtypeStruct((n_seq, N_HEADS, D_HEAD, D_HEAD), F32),
    ]
    kern = functools.partial(_seq_kernel, tc_in=tc_in, valid_hi=valid_hi, front_pad=front_pad,
                             has_init=init is not None, precise=precise)
    return pl.pallas_call(
        kern,
        grid=(n_seq, n_chunks),
        in_specs=in_specs,
        out_specs=out_specs,
        out_shape=out_shape,
        scratch_shapes=[pltpu.VMEM((CHUNK + 8, A_QKV), F32), pltpu.VMEM((CHUNK, A_QKV), F32),
                        pltpu.VMEM((1, LANES), F32)],
        compiler_params=_params(2),
        name=("seq_prompt" if init is None else "seq_sample") + ("_split" if precise else ""),
    )(*args)


def _fox_prompt_kernel(q_ref, k_ref, v_ref, ct_ref, o_ref, m_ref, l_ref, acc_ref, *, tb, precise):
    mm = functools.partial(_dot, precise=precise)
    i = pl.program_id(1)
    j = pl.program_id(2)

    @pl.when(j == 0)
    def _():
        m_ref[...] = jnp.full_like(m_ref, NEG)
        l_ref[...] = jnp.zeros_like(l_ref)
        acc_ref[...] = jnp.zeros_like(acc_ref)

    @pl.when(j <= i)
    def _():
        qpos = i * tb + lax.broadcasted_iota(jnp.int32, (tb, tb), 0)
        kpos = j * tb + lax.broadcasted_iota(jnp.int32, (tb, tb), 1)
        ok = (kpos <= qpos) & (kpos >= PAD_FRONT)
        for h in range(N_HEADS):
            hs = slice(h * D_HEAD, (h + 1) * D_HEAD)
            s = mm(q_ref[:, hs], k_ref[:, hs], NT) * (D_HEAD ** -0.5) - ct_ref[0, h:h + 1, :]
            s = jnp.where(ok, s, NEG)
            m_old = m_ref[h][:, 0:1]
            m_new = jnp.maximum(m_old, jnp.max(s, axis=-1, keepdims=True))
            alpha = jnp.exp(m_old - m_new)
            p = jnp.exp(s - m_new)
            l_ref[h] = alpha * l_ref[h] + jnp.sum(p, axis=-1, keepdims=True)
            acc_ref[h] = alpha * acc_ref[h] + mm(p, v_ref[:, hs])
            m_ref[h] = jnp.broadcast_to(m_new, (tb, LANES))

    @pl.when(j == i)
    def _():
        qvalid = (i * tb + lax.broadcasted_iota(jnp.int32, (tb, D_HEAD), 0)) >= PAD_FRONT
        for h in range(N_HEADS):
            o = acc_ref[h] / l_ref[h][:, 0:1]
            o_ref[:, h * D_HEAD:(h + 1) * D_HEAD] = jnp.where(qvalid, o, 0.0)


def _fox_prompt(p, ct, n_seq, lp, tb, precise):
    r = p.shape[0]
    nb = lp // tb
    return pl.pallas_call(
        functools.partial(_fox_prompt_kernel, tb=tb, precise=precise),
        grid=(n_seq, nb, nb),
        in_specs=[
            pl.BlockSpec((tb, BRANCH_W), lambda n, i, j: (n * nb + i, COL_BQ)),
            pl.BlockSpec((tb, BRANCH_W), lambda n, i, j: (n * nb + jnp.minimum(i, j), COL_BK)),
            pl.BlockSpec((tb, BRANCH_W), lambda n, i, j: (n * nb + jnp.minimum(i, j), COL_BV)),
            pl.BlockSpec((1, 8, tb), lambda n, i, j: (n, 0, jnp.minimum(i, j))),
        ],
        out_specs=pl.BlockSpec((tb, BRANCH_W), lambda n, i, j: (n * nb + i, 0)),
        out_shape=jax.ShapeDtypeStruct((r, BRANCH_W), F32),
        scratch_shapes=[pltpu.VMEM((N_HEADS, tb, LANES), F32), pltpu.VMEM((N_HEADS, tb, LANES), F32),
                        pltpu.VMEM((N_HEADS, tb, D_HEAD), F32)],
        compiler_params=_params(3),
        name="fox_prompt",
    )(p, p, p, ct)


def _fox_sample_kernel(*refs, ppg, t_valid, precise):
    q_ref, kn_ref, vn_ref, lfn_ref = refs[1:5]
    k_refs = refs[5:5 + ppg]
    v_refs = refs[5 + ppg:5 + 2 * ppg]
    lf_refs = refs[5 + 2 * ppg:5 + 3 * ppg]
    o_ref, m_ref, l_ref, acc_ref, carry = refs[5 + 3 * ppg:]
    mm = functools.partial(_dot, precise=precise)
    g = pl.program_id(1)
    ng = pl.num_programs(1)

    @pl.when(g == 0)
    def _():
        m_ref[...] = jnp.full_like(m_ref, NEG)
        l_ref[...] = jnp.zeros_like(l_ref)
        acc_ref[...] = jnp.zeros_like(acc_ref)
        carry[...] = jnp.zeros_like(carry)

    ri = lax.broadcasted_iota(jnp.int32, (CHUNK, CHUNK), 0)
    ci = lax.broadcasted_iota(jnp.int32, (CHUNK, CHUNK), 1)
    triu = jnp.where(ri <= ci, 1.0, 0.0).astype(BF16)

    def cumsum_lanes(x):
        x1, x2, x3 = _split3(x)
        return (jnp.dot(x1, triu, preferred_element_type=F32) + jnp.dot(x2, triu, preferred_element_type=F32)
                + jnp.dot(x3, triu, preferred_element_type=F32))

    def update(h, s, vh):
        m_old = m_ref[h][:, 0:1]
        m_new = jnp.maximum(m_old, jnp.max(s, axis=-1, keepdims=True))
        alpha = jnp.exp(m_old - m_new)
        p = jnp.exp(s - m_new)
        l_ref[h] = alpha * l_ref[h] + jnp.sum(p, axis=-1, keepdims=True)
        acc_ref[h] = alpha * acc_ref[h] + mm(p, vh)
        m_ref[h] = jnp.broadcast_to(m_new, (SAMPLE_ROWS, LANES))

    scale = D_HEAD ** -0.5
    q_heads = [_split2(q_ref[:, h * D_HEAD:(h + 1) * D_HEAD]) for h in range(N_HEADS)]
    for pg in range(ppg):
        cin = cumsum_lanes(lf_refs[pg][0]) + carry[...]
        carry[...] = jnp.broadcast_to(cin[:, CHUNK - 1:CHUNK], (8, LANES))
        for h in range(N_HEADS):
            hs = slice(h * D_HEAD, (h + 1) * D_HEAD)
            s = mm(q_heads[h], k_refs[pg][0, :, hs], NT) * scale - cin[h:h + 1, :]
            update(h, s, v_refs[pg][0, :, hs])

    @pl.when(g == ng - 1)
    def _():
        cn = cumsum_lanes(lfn_ref[0]) + carry[...]
        qi = lax.broadcasted_iota(jnp.int32, (SAMPLE_ROWS, CHUNK), 0)
        ki = lax.broadcasted_iota(jnp.int32, (SAMPLE_ROWS, CHUNK), 1)
        ok = (ki <= qi) & (ki < t_valid)
        zpad = jnp.zeros((CHUNK - SAMPLE_ROWS, D_HEAD), F32)
        qvalid = lax.broadcasted_iota(jnp.int32, (SAMPLE_ROWS, D_HEAD), 0) < t_valid
        for h in range(N_HEADS):
            hs = slice(h * D_HEAD, (h + 1) * D_HEAD)
            kh = jnp.concatenate([kn_ref[:, hs], zpad], axis=0)
            vh = jnp.concatenate([vn_ref[:, hs], zpad], axis=0)
            s = mm(q_heads[h], kh, NT) * scale - cn[h:h + 1, :]
            update(h, jnp.where(ok, s, NEG), vh)
            o = acc_ref[h] / l_ref[h][:, 0:1]
            o_ref[:, hs] = jnp.where(qvalid, o, 0.0)


def _fox_sample(page_table, p, lfn_t, cache_k, cache_v, cache_lf_t, *, layer, n_pool, ppg, t_valid, precise):
    r = p.shape[0]
    n_seq, n_pages = page_table.shape
    base = layer * n_pool
    pt = page_table.reshape(-1)

    def page(pg):
        return lambda n, g, pt_ref: (base + pt_ref[n * n_pages + g * ppg + pg], 0, 0)

    in_specs = [
        pl.BlockSpec((SAMPLE_ROWS, BRANCH_W), lambda n, g, pt_ref: (n, COL_BQ)),
        pl.BlockSpec((SAMPLE_ROWS, BRANCH_W), lambda n, g, pt_ref: (n, COL_BK)),
        pl.BlockSpec((SAMPLE_ROWS, BRANCH_W), lambda n, g, pt_ref: (n, COL_BV)),
        pl.BlockSpec((1, 8, LANES), lambda n, g, pt_ref: (n, 0, 0)),
    ]
    in_specs += [pl.BlockSpec((1, CHUNK, BRANCH_W), page(pg)) for pg in range(ppg)]
    in_specs += [pl.BlockSpec((1, CHUNK, BRANCH_W), page(pg)) for pg in range(ppg)]
    in_specs += [pl.BlockSpec((1, 8, CHUNK), page(pg)) for pg in range(ppg)]
    grid_spec = pltpu.PrefetchScalarGridSpec(
        num_scalar_prefetch=1,
        grid=(n_seq, n_pages // ppg),
        in_specs=in_specs,
        out_specs=pl.BlockSpec((SAMPLE_ROWS, BRANCH_W), lambda n, g, pt_ref: (n, 0)),
        scratch_shapes=[pltpu.VMEM((N_HEADS, SAMPLE_ROWS, LANES), F32),
                        pltpu.VMEM((N_HEADS, SAMPLE_ROWS, LANES), F32),
                        pltpu.VMEM((N_HEADS, SAMPLE_ROWS, D_HEAD), F32),
                        pltpu.VMEM((8, LANES), F32)],
    )
    return pl.pallas_call(
        functools.partial(_fox_sample_kernel, ppg=ppg, t_valid=t_valid, precise=precise),
        grid_spec=grid_spec,
        out_shape=jax.ShapeDtypeStruct((r, BRANCH_W), F32),
        compiler_params=_params(2),
        name="fox_sample",
    )(pt, p, p, p, lfn_t, *([cache_k] * ppg), *([cache_v] * ppg), *([cache_lf_t] * ppg))


def _merge_kernel(oa_ref, ob_ref, oc_ref, gates_ref, x_ref, wb_ref, wo_ref, nf_ref, wr_ref, br_ref,
                  x1_ref, xn_ref, gate_ref, *, precise):
    mm = functools.partial(_dot, precise=precise)
    tm = x_ref.shape[0]
    merged = None
    for b, o_ref in enumerate((oa_ref, ob_ref, oc_ref)):
        term = mm(o_ref[...], wb_ref[b]) * _sigmoid(gates_ref[:, b * D_MODEL:(b + 1) * D_MODEL])
        merged = term if merged is None else merged + term
    x1 = x_ref[...] + mm(merged, wo_ref[...])
    x1_ref[...] = x1
    xn = _split2(_rms(x1, nf_ref[...]))
    xn_ref[...] = xn[0]
    logits = _dot(xn, wr_ref[...], precise=True) + br_ref[...]

    lane = lax.broadcasted_iota(jnp.int32, (tm, LANES), 1).astype(F32)
    big = float(LANES)
    is_grp = lane < float(N_GROUPS)
    mx = jnp.max(jnp.where(is_grp, logits, NEG), axis=-1, keepdims=True)
    p_grp = 1.0 / jnp.sum(jnp.where(is_grp, jnp.exp(jnp.where(is_grp, logits - mx, 0.0)), 0.0),
                          axis=-1, keepdims=True)
    g_star = jnp.min(jnp.where(is_grp & (logits == mx), lane, big), axis=-1, keepdims=True)
    base = float(LANE_EXPERT) + float(EXP_PER_GROUP) * g_star
    is_exp = (lane >= base) & (lane < base + float(EXP_PER_GROUP))
    le = jnp.where(is_exp, logits, NEG)
    t1 = jnp.max(le, axis=-1, keepdims=True)
    i1 = jnp.min(jnp.where(is_exp & (le == t1), lane, big), axis=-1, keepdims=True)
    le2 = jnp.where(lane == i1, NEG, le)
    t2 = jnp.max(le2, axis=-1, keepdims=True)
    i2 = jnp.min(jnp.where(is_exp & (lane != i1) & (le2 == t2), lane, big), axis=-1, keepdims=True)
    e2 = jnp.exp(t2 - t1)
    w1 = p_grp / (1.0 + e2)
    w2 = w1 * e2
    gate_ref[...] = jnp.where(lane == i1, w1, jnp.where(lane == i2, w2, 0.0))


def _merge(oa, ob, oc, p, x, wb, wo, nf, wr, br, tm, precise):
    r = x.shape[0]
    rows = lambda i: (i, 0)
    const2 = lambda i: (0, 0)
    return pl.pallas_call(
        functools.partial(_merge_kernel, precise=precise),
        grid=(r // tm,),
        in_specs=[
            pl.BlockSpec((tm, BRANCH_W), rows),
            pl.BlockSpec((tm, BRANCH_W), rows),
            pl.BlockSpec((tm, BRANCH_W), rows),
            pl.BlockSpec((tm, N_BRANCH * D_MODEL), rows),
            pl.BlockSpec((tm, D_MODEL), rows),
            pl.BlockSpec((N_BRANCH, BRANCH_W, D_MODEL), lambda i: (0, 0, 0)),
            pl.BlockSpec((D_MODEL, D_MODEL), const2),
            pl.BlockSpec((1, D_MODEL), const2),
            pl.BlockSpec((D_MODEL, LANES), const2),
            pl.BlockSpec((1, LANES), const2),
        ],
        out_specs=[
            pl.BlockSpec((tm, D_MODEL), rows),
            pl.BlockSpec((tm, D_MODEL), rows),
            pl.BlockSpec((tm, LANES), rows),
        ],
        out_shape=[jax.ShapeDtypeStruct((r, D_MODEL), F32), jax.ShapeDtypeStruct((r, D_MODEL), BF16),
                   jax.ShapeDtypeStruct((r, LANES), F32)],
        compiler_params=_params(1),
        name="merge_split" if precise else "merge",
    )(oa, ob, oc, p, x, wb, wo, nf, wr, br)


def _moe_kernel(xn_ref, gate_ref, x1_ref, w1_ref, w3_ref, w2_ref, nfin_ref, *out_refs, final):
    x2_ref = out_refs[0]
    e = pl.program_id(1)

    @pl.when(e == 0)
    def _():
        x2_ref[...] = x1_ref[...]

    xn = xn_ref[...]
    lane = lax.broadcasted_iota(jnp.int32, gate_ref.shape, 1)
    g_e = jnp.sum(jnp.where(lane == LANE_EXPERT + e, gate_ref[...], 0.0), axis=-1, keepdims=True)
    h = _silu(_dot(xn, w1_ref[0])) * _dot(xn, w3_ref[0])
    x2_ref[...] += _dot(h * g_e, w2_ref[0])

    if final:
        @pl.when(e == pl.num_programs(1) - 1)
        def _():
            out_refs[1][...] = _rms(x2_ref[...], nfin_ref[...])


def _moe(xn, gate, x1, w1, w3, w2, nfin, tm, final):
    r = x1.shape[0]
    rows = lambda i, e: (i, 0)
    out_specs = [pl.BlockSpec((tm, D_MODEL), rows)]
    out_shape = [jax.ShapeDtypeStruct((r, D_MODEL), F32)]
    if final:
        out_specs.append(pl.BlockSpec((tm, D_MODEL), rows))
        out_shape.append(jax.ShapeDtypeStruct((r, D_MODEL), F32))
    return pl.pallas_call(
        functools.partial(_moe_kernel, final=final),
        grid=(r // tm, N_EXPERTS),
        in_specs=[
            pl.BlockSpec((tm, D_MODEL), rows),
            pl.BlockSpec((tm, LANES), rows),
            pl.BlockSpec((tm, D_MODEL), rows),
            pl.BlockSpec((1, D_MODEL, D_FF), lambda i, e: (e, 0, 0)),
            pl.BlockSpec((1, D_MODEL, D_FF), lambda i, e: (e, 0, 0)),
            pl.BlockSpec((1, D_FF, D_MODEL), lambda i, e: (e, 0, 0)),
            pl.BlockSpec((1, D_MODEL), lambda i, e: (0, 0)),
        ],
        out_specs=out_specs,
        out_shape=out_shape,
        compiler_params=_params(2),
        name="moe",
    )(xn, gate, x1, w1, w3, w2, nfin)


def _rope_tables(pos):
    half = D_HEAD // 2
    inv = jnp.asarray((ROPE_BASE ** (-np.arange(half, dtype=np.float64) / half)).astype(np.float32))
    ang = pos.astype(F32)[:, None] * inv[None, :]
    cos, sin = jnp.cos(ang), jnp.sin(ang)
    return jnp.concatenate([cos, cos], axis=-1), jnp.concatenate([-sin, sin], axis=-1)


def _retention_tables(t_eff):
    idx = jnp.arange(CHUNK, dtype=F32)
    log_gamma = jnp.log(1.0 - jnp.exp2(-5.0 - jnp.arange(N_HEADS, dtype=F32)))
    incl = idx[:, None] >= idx[None, :]
    rel = jnp.where(incl, idx[:, None] - idx[None, :], 0.0)
    dmat = jnp.where(incl, jnp.exp(rel[None] * log_gamma[:, None, None]), 0.0)
    cross = jnp.exp((idx + 1.0)[:, None] * log_gamma[None, :])
    kdec = jnp.exp((t_eff - 1.0 - idx)[:, None] * log_gamma[None, :])
    tot = jnp.broadcast_to(jnp.exp(t_eff * log_gamma)[None, :], (CHUNK, N_HEADS))
    rtab = jnp.concatenate([cross, kdec, tot, jnp.zeros((CHUNK, LANES - 3 * N_HEADS), F32)], axis=1)
    return dmat, rtab


def _lanes_row(pairs):
    row = jnp.zeros((LANES,), F32)
    for start, vals in pairs:
        row = row.at[start:start + vals.shape[0]].set(vals.astype(F32))
    return row


def _pick(n, candidates):
    for c in candidates:
        if n % c == 0:
            return c
    raise ValueError(f"no tile in {candidates} divides {n}")


def kernel(x_prompt, x_sample, cache_k, cache_v, cache_logf, page_table, state_delta, state_conv, state_ret,
           meta_tokens, norm_mix, norm_ffn, norm_final, w_in, conv_w, a_log, dt_bias, a_norm, b_fbias, c_norm,
           w_branch, w_out, w_router_group, b_router_group, w_router_expert, b_router_expert, w1, w3, w2):
    nb, seq, _ = x_prompt.shape
    ns, ts, _ = x_sample.shape
    depth, n_pool = cache_k.shape[:2]
    n_pages = page_table.shape[1]
    assert seq % CHUNK == 0 and ts <= SAMPLE_ROWS and cache_k.shape[2] == CHUNK
    lp = CHUNK + seq
    n_chunks = lp // CHUNK
    rp = nb * lp
    rs = ns * SAMPLE_ROWS
    tm_p = _pick(rp, (512, 256, 128))
    tm_s = _pick(rs, (256, 128, 64, 32, 16, 8))

    xp = jnp.concatenate([jnp.zeros((nb, PAD_FRONT, D_MODEL), F32),
                          jnp.broadcast_to(meta_tokens[None], (nb, N_META, D_MODEL)), x_prompt],
                         axis=1).reshape(rp, D_MODEL)
    xs = jnp.concatenate([x_sample, jnp.zeros((ns, SAMPLE_ROWS - ts, D_MODEL), F32)], axis=1).reshape(rs, D_MODEL)

    cos_p, sin_p = _rope_tables(jnp.arange(lp) - PAD_FRONT)
    cos_s, sin_s = _rope_tables(n_pages * CHUNK + jnp.arange(CHUNK))
    dmat_p, rtab_p = _retention_tables(float(CHUNK))
    dmat_s, rtab_s = _retention_tables(float(ts))

    cache_k2 = cache_k.reshape(depth * n_pool, CHUNK, BRANCH_W)
    cache_v2 = cache_v.reshape(depth * n_pool, CHUNK, BRANCH_W)
    cache_lf_t = jnp.pad(jnp.swapaxes(cache_logf, 2, 3).reshape(depth * n_pool, N_HEADS, CHUNK),
                         ((0, 0), (0, 8 - N_HEADS), (0, 0)))
    tb = _pick(lp, (384, 256, 128))
    ppg = _pick(n_pages, (8, 4, 2, 1))

    o_beta = A_QKV + BRANCH_W
    o_bq = o_beta + 8
    o_f = o_bq + 3 * BRANCH_W
    o_gates = o_f + 4 + 4 * BRANCH_W
    outs_p = [[] for _ in range(6)]
    outs_s = [[] for _ in range(6)]
    yp = ys = None
    for l in range(depth):
        final = l == depth - 1
        hp = not final
        wdt = F32 if hp else BF16
        wl = w_in[l]
        w_main = jnp.concatenate([wl[:, o_gates:], wl[:, :o_beta], wl[:, o_bq:o_f], wl[:, o_f + 4:o_gates]],
                                 axis=1).astype(wdt)
        w_small = jnp.concatenate([wl[:, o_beta:o_bq], wl[:, o_f:o_f + 4],
                                   jnp.zeros((D_MODEL, LANES - 12), F32)], axis=1).astype(wdt)
        g_mix = norm_mix[l][None]
        p_p, ps_p = _proj(xp, g_mix, w_main, w_small, tm_p, 512, hp)
        p_s, ps_s = _proj(xs, g_mix, w_main, w_small, tm_s, 512, hp)

        avec = jnp.zeros((8, LANES), F32)
        avec = avec.at[0].set(_lanes_row([(LANE_ALPHA, a_log[l])]))
        avec = avec.at[1].set(_lanes_row([(LANE_ALPHA, dt_bias[l])]))
        avec = avec.at[2].set(_lanes_row([(LANE_F, b_fbias[l])]))
        shared = (conv_w[l], avec, a_norm[l][None], c_norm[l][None])
        oa_p, oc_p, lf_p, cc_p, sd_p, sr_p = _seq(
            p_p, ps_p, cos_p, sin_p, *shared, dmat_p, rtab_p, n_seq=nb, n_chunks=n_chunks, tc_in=CHUNK,
            valid_hi=CHUNK, front_pad=PAD_FRONT, precise=hp)
        cb0 = jnp.pad(state_conv[l], ((0, 0), (8 - (CONV_W - 1), 0), (0, 0)))
        oa_s, oc_s, lf_s, cc_s, sd_s, sr_s = _seq(
            p_s, ps_s, cos_s, sin_s, *shared, dmat_s, rtab_s, n_seq=ns, n_chunks=1, tc_in=SAMPLE_ROWS,
            valid_hi=ts, front_pad=0, precise=hp, init=(state_delta[l], state_ret[l], cb0))

        ct = jnp.pad(jnp.swapaxes(cc_p[:, LANE_F:LANE_F + N_HEADS].reshape(nb, lp, N_HEADS), 1, 2),
                     ((0, 0), (0, 8 - N_HEADS), (0, 0)))
        ob_p = _fox_prompt(p_p, ct, nb, lp, tb, hp)
        lfn = lf_s[:, LANE_F:LANE_F + N_HEADS].reshape(ns, SAMPLE_ROWS, N_HEADS)
        lfn_t = jnp.pad(jnp.swapaxes(lfn, 1, 2), ((0, 0), (0, 8 - N_HEADS), (0, LANES - SAMPLE_ROWS)))
        ob_s = _fox_sample(page_table, p_s, lfn_t, cache_k2, cache_v2, cache_lf_t, layer=l, n_pool=n_pool,
                           ppg=ppg, t_valid=ts, precise=hp)

        wr = jnp.concatenate([w_router_group[l], w_router_expert[l],
                              jnp.zeros((D_MODEL, LANES - N_GROUPS - N_EXPERTS), F32)], axis=1)
        br = _lanes_row([(LANE_GROUP, b_router_group[l]), (LANE_EXPERT, b_router_expert[l].reshape(-1))])[None]
        nf = norm_ffn[l][None]
        wb, wo = w_branch[l].astype(wdt), w_out[l].astype(wdt)
        x1_p, xn_p, gate_p = _merge(oa_p, ob_p, oc_p, p_p, xp, wb, wo, nf, wr, br, min(tm_p, 256), hp)
        x1_s, xn_s, gate_s = _merge(oa_s, ob_s, oc_s, p_s, xs, wb, wo, nf, wr, br, tm_s, hp)

        ew1 = w1[l].reshape(N_EXPERTS, D_MODEL, D_FF).astype(BF16)
        ew3 = w3[l].reshape(N_EXPERTS, D_MODEL, D_FF).astype(BF16)
        ew2 = w2[l].reshape(N_EXPERTS, D_FF, D_MODEL).astype(BF16)
        res_p = _moe(xn_p, gate_p, x1_p, ew1, ew3, ew2, norm_final[None], tm_p, final)
        res_s = _moe(xn_s, gate_s, x1_s, ew1, ew3, ew2, norm_final[None], tm_s, final)
        xp, xs = res_p[0], res_s[0]
        if final:
            yp, ys = res_p[1], res_s[1]

        def prompt_rows(a):
            return a.reshape(nb, lp, a.shape[-1])[:, PAD_FRONT:]

        def sample_rows(a):
            return a.reshape(ns, SAMPLE_ROWS, a.shape[-1])[:, :ts]

        def col(a, c, w=BRANCH_W):
            return a[:, c * BRANCH_W:c * BRANCH_W + w]

        for lst, rows_of, p, lf_a, sd, sr in ((outs_p, prompt_rows, p_p, lf_p, sd_p, sr_p),
                                              (outs_s, sample_rows, p_s, lf_s, sd_s, sr_s)):
            k_new = rows_of(col(p, COL_BK))
            v_new = rows_of(col(p, COL_BV))
            lst[0].append(k_new.reshape(k_new.shape[:2] + (N_HEADS, D_HEAD)))
            lst[1].append(v_new.reshape(v_new.shape[:2] + (N_HEADS, D_HEAD)))
            lst[2].append(rows_of(lf_a[:, LANE_F:LANE_F + N_HEADS]))
            lst[3].append(sd)
            lst[4].append(rows_of(col(p, COL_AQKV, A_QKV))[:, -(CONV_W - 1):])
            lst[5].append(sr)

    y_prompt = yp.reshape(nb, lp, D_MODEL)[:, CHUNK:]
    y_sample = ys.reshape(ns, SAMPLE_ROWS, D_MODEL)[:, :ts]
    return (y_prompt, y_sample, *(jnp.stack(a) for a in outs_p), *(jnp.stack(a) for a in outs_s))
```

```python
import functools

import jax
import jax.numpy as jnp
import numpy as np
from jax import lax
from jax.experimental import pallas as pl
from jax.experimental.pallas import tpu as pltpu

F32 = jnp.float32
BF16 = jnp.bfloat16

D_MODEL = 1024
N_META = 16
CHUNK = 128
PAD_FRONT = CHUNK - N_META
N_HEADS = 4
D_HEAD = 128
BRANCH_W = N_HEADS * D_HEAD
A_QKV = 3 * BRANCH_W
CONV_W = 4
N_BRANCH = 3
N_GROUPS = 4
EXP_PER_GROUP = 4
N_EXPERTS = N_GROUPS * EXP_PER_GROUP
D_FF = 512
ROPE_BASE = 10000.0
EPS = 1e-6
NEG = -1e30
SAMPLE_ROWS = 8
LANES = 128

COL_GATES = 0
COL_AQKV = 6
COL_AZ = 9
COL_BQ, COL_BK, COL_BV = 10, 11, 12
COL_CQ, COL_CK, COL_CV, COL_CG = 13, 14, 15, 16
N_MAIN = 17 * BRANCH_W
LANE_BETA, LANE_ALPHA, LANE_F = 0, 4, 8
LANE_GROUP, LANE_EXPERT = 0, 4

VMEM_LIMIT = 56 * 1024 * 1024

NN = (((1,), (0,)), ((), ()))
NT = (((1,), (1,)), ((), ()))
TN = (((0,), (0,)), ((), ()))


def _split2(x):
    hi = x.astype(BF16)
    return hi, (x - hi.astype(F32)).astype(BF16)


def _dot(a, b, dims=NN, precise=False):
    def dg(x, y):
        return lax.dot_general(x, y, dims, preferred_element_type=F32)

    if not precise:
        a = a[0] if isinstance(a, tuple) else a.astype(BF16)
        b = b[0] if isinstance(b, tuple) else b.astype(BF16)
        return dg(a, b)
    a_hi, a_lo = a if isinstance(a, tuple) else _split2(a)
    b_hi, b_lo = b if isinstance(b, tuple) else _split2(b)
    return dg(a_hi, b_hi) + dg(a_hi, b_lo) + dg(a_lo, b_hi)


def _split3(x):
    x1 = x.astype(BF16)
    r1 = x - x1.astype(F32)
    x2 = r1.astype(BF16)
    x3 = (r1 - x2.astype(F32)).astype(BF16)
    return x1, x2, x3


def _sigmoid(x):
    return 1.0 / (1.0 + jnp.exp(-x))


def _silu(x):
    return x * _sigmoid(x)


def _softplus(x):
    return jnp.maximum(x, 0.0) + jnp.log1p(jnp.exp(-jnp.abs(x)))


def _rms(x, g):
    return x * lax.rsqrt(jnp.mean(x * x, axis=-1, keepdims=True) + EPS) * g


def _params(n_axes):
    return pltpu.CompilerParams(dimension_semantics=("arbitrary",) * n_axes, vmem_limit_bytes=VMEM_LIMIT)


def _proj_kernel(x_ref, g_ref, w_ref, ws_ref, p_ref, ps_ref, xn_ref, *, precise):
    @pl.when(pl.program_id(1) == 0)
    def _():
        xn = _split2(_rms(x_ref[...], g_ref[...]))
        xn_ref[0] = xn[0]
        if precise:
            xn_ref[1] = xn[1]
        ps_ref[...] = _dot(xn, ws_ref[...], precise=precise)

    xn = (xn_ref[0], xn_ref[1]) if precise else xn_ref[0]
    p_ref[...] = _dot(xn, w_ref[...], precise=precise)


def _proj(x, g, w_main, w_small, tm, tn, precise):
    r = x.shape[0]
    return pl.pallas_call(
        functools.partial(_proj_kernel, precise=precise),
        grid=(r // tm, N_MAIN // tn),
        in_specs=[pl.BlockSpec((tm, D_MODEL), lambda i, j: (i, 0)),
                  pl.BlockSpec((1, D_MODEL), lambda i, j: (0, 0)),
                  pl.BlockSpec((D_MODEL, tn), lambda i, j: (0, j)),
                  pl.BlockSpec((D_MODEL, LANES), lambda i, j: (0, 0))],
        out_specs=[pl.BlockSpec((tm, tn), lambda i, j: (i, j)),
                   pl.BlockSpec((tm, LANES), lambda i, j: (i, 0))],
        out_shape=[jax.ShapeDtypeStruct((r, N_MAIN), F32), jax.ShapeDtypeStruct((r, LANES), F32)],
        scratch_shapes=[pltpu.VMEM((2 if precise else 1, tm, D_MODEL), BF16)],
        compiler_params=_params(2),
        name="proj_split" if precise else "proj",
    )(x, g, w_main, w_small)


def _delta_head(q, k, v, beta, gcol, grow, s, incl, strict, precise):
    mm = functools.partial(_dot, precise=precise)
    eg = jnp.exp(gcol)
    decay = jnp.where(incl, jnp.exp(jnp.where(incl, gcol - grow, 0.0)), 0.0)
    kb = k * beta
    m = jnp.where(strict, mm(kb, k, NT) * decay, 0.0)
    npow = -m
    r = npow
    for _ in range(6):
        npow = mm(npow, npow)
        r = r + npow + mm(r, npow)
    e = -(m + r + _dot(m, r, precise=True))
    r = r + e + mm(r, e)
    vb = v * beta
    kbe = kb * eg
    u = vb + mm(r, vb)
    w = kbe + mm(r, kbe)
    v_new = u - mm(w, s)
    attn = mm(q, k, NT) * decay
    o = mm(q * eg, s) + mm(attn, v_new)
    g_last = gcol[CHUNK - 1:CHUNK, :]
    s_new = s * jnp.exp(g_last) + mm(k * jnp.exp(g_last - gcol), v_new, TN)
    return o, s_new


def _seq_kernel(*refs, tc_in, valid_hi, front_pad, has_init, precise):
    if has_init:
        (aqkv_ref, az_ref, cq_ref, ck_ref, cv_ref, cg_ref, sm_ref, cos_ref, sin_ref,
         convw_ref, avec_ref, anorm_ref, cnorm_ref, dmat_ref, rtab_ref,
         sd0_ref, sr0_ref, cb0_ref,
         oa_ref, oc_ref, lf_ref, cc_ref, sd_ref, sr_ref, xbuf, cbuf, carry) = refs
    else:
        (aqkv_ref, az_ref, cq_ref, ck_ref, cv_ref, cg_ref, sm_ref, cos_ref, sin_ref,
         convw_ref, avec_ref, anorm_ref, cnorm_ref, dmat_ref, rtab_ref,
         oa_ref, oc_ref, lf_ref, cc_ref, sd_ref, sr_ref, xbuf, cbuf, carry) = refs
    mm = functools.partial(_dot, precise=precise)
    c = pl.program_id(1)

    def ext(ref, cols=None):
        v = ref[...] if cols is None else ref[:, cols]
        if tc_in == CHUNK:
            return v
        return jnp.concatenate([v, jnp.zeros((CHUNK - tc_in, v.shape[1]), v.dtype)], axis=0)

    @pl.when(c == 0)
    def _():
        carry[...] = jnp.zeros_like(carry)
        if has_init:
            xbuf[0:8, :] = cb0_ref[0]
            sd_ref[...] = sd0_ref[...]
            sr_ref[...] = sr0_ref[...]
        else:
            xbuf[0:8, :] = jnp.zeros((8, A_QKV), F32)
            sd_ref[...] = jnp.zeros_like(sd_ref)
            sr_ref[...] = jnp.zeros_like(sr_ref)

    rows = lax.broadcasted_iota(jnp.int32, (CHUNK, LANES), 0)
    lane = lax.broadcasted_iota(jnp.int32, (CHUNK, LANES), 1)
    lo = jnp.where(c == 0, front_pad, 0)
    valid = (rows >= lo) & (rows < valid_hi)
    incl = rows >= lane
    strict = rows > lane

    for j in range(A_QKV // LANES):
        cols = slice(j * LANES, (j + 1) * LANES)
        xbuf[8:8 + CHUNK, cols] = ext(aqkv_ref, cols)
        acc = xbuf[5:5 + CHUNK, cols] * convw_ref[0:1, cols]
        for i in range(1, CONV_W):
            acc = acc + xbuf[5 + i:5 + i + CHUNK, cols] * convw_ref[i:i + 1, cols]
        cbuf[:, cols] = jnp.where(valid, _silu(acc), 0.0)
        xbuf[0:8, cols] = xbuf[CHUNK:CHUNK + 8, cols]

    sm = ext(sm_ref)
    av = avec_ref[...]
    g_all = -jnp.exp(av[0:1, :]) * _softplus(sm + av[1:2, :])
    lf_all = -_softplus(-(sm + av[2:3, :]))
    is_g = (lane >= LANE_ALPHA) & (lane < LANE_ALPHA + N_HEADS)
    is_f = (lane >= LANE_F) & (lane < LANE_F + N_HEADS)
    z = jnp.where(valid & is_g, g_all, jnp.where(valid & is_f, lf_all, 0.0))
    tril = jnp.where(incl, 1.0, 0.0).astype(BF16)
    z1, z2, z3 = _split3(z)
    cum = (jnp.dot(tril, z1, preferred_element_type=F32) + jnp.dot(tril, z2, preferred_element_type=F32)
           + jnp.dot(tril, z3, preferred_element_type=F32))
    cum_t = cum.T
    cc = cum + carry[...]
    carry[...] = cc[CHUNK - 1:CHUNK, :]
    lf_ref[...] = z[:tc_in]
    cc_ref[...] = cc[:tc_in]

    beta_all = _sigmoid(sm)
    cos2 = cos_ref[...]
    sin2 = sin_ref[...]
    rtab = rtab_ref[...]
    for h in range(N_HEADS):
        hs = slice(h * D_HEAD, (h + 1) * D_HEAD)
        q = cbuf[:, hs]
        k = cbuf[:, BRANCH_W + h * D_HEAD:BRANCH_W + (h + 1) * D_HEAD]
        v = cbuf[:, 2 * BRANCH_W + h * D_HEAD:2 * BRANCH_W + (h + 1) * D_HEAD]
        q = q * lax.rsqrt(jnp.sum(q * q, axis=-1, keepdims=True) + EPS) * (D_HEAD ** -0.5)
        k = k * lax.rsqrt(jnp.sum(k * k, axis=-1, keepdims=True) + EPS)
        beta = beta_all[:, LANE_BETA + h:LANE_BETA + h + 1]
        gcol = cum[:, LANE_ALPHA + h:LANE_ALPHA + h + 1]
        grow = cum_t[LANE_ALPHA + h:LANE_ALPHA + h + 1, :]
        o, s_new = _delta_head(q, k, v, beta, gcol, grow, sd_ref[0, h], incl, strict, precise)
        sd_ref[0, h] = s_new
        oa = _rms(o, anorm_ref[...]) * _silu(ext(az_ref, hs))
        oa_ref[:, hs] = oa[:tc_in]

        qc = ext(cq_ref, hs)
        kc = ext(ck_ref, hs)
        vc = ext(cv_ref, hs)
        qc = qc * cos2 + pltpu.roll(qc, D_HEAD // 2, 1) * sin2
        kc = (kc * cos2 + pltpu.roll(kc, D_HEAD // 2, 1) * sin2) * (D_HEAD ** -0.5)
        s_r = sr_ref[0, h]
        inner = mm(mm(qc, kc, NT) * dmat_ref[h], vc)
        cross = mm(qc * rtab[:, h:h + 1], s_r)
        sr_ref[0, h] = s_r * rtab[0:1, 8 + h:9 + h] + mm(kc * rtab[:, 4 + h:5 + h], vc, TN)
        oc = _rms(inner + cross, cnorm_ref[...]) * _silu(ext(cg_ref, hs))
        oc_ref[:, hs] = oc[:tc_in]


def _seq(p, ps, cos2, sin2, convw, avec, anorm, cnorm, dmat, rtab, *, n_seq, n_chunks, tc_in,
         valid_hi, front_pad, precise, init=None):
    r = p.shape[0]

    def rows(col):
        return lambda n, c: (n * n_chunks + c, col)

    const2 = lambda n, c: (0, 0)
    state_spec = pl.BlockSpec((1, N_HEADS, D_HEAD, D_HEAD), lambda n, c: (n, 0, 0, 0))
    in_specs = [
        pl.BlockSpec((tc_in, A_QKV), rows(COL_AQKV // 3)),
        pl.BlockSpec((tc_in, BRANCH_W), rows(COL_AZ)),
        pl.BlockSpec((tc_in, BRANCH_W), rows(COL_CQ)),
        pl.BlockSpec((tc_in, BRANCH_W), rows(COL_CK)),
        pl.BlockSpec((tc_in, BRANCH_W), rows(COL_CV)),
        pl.BlockSpec((tc_in, BRANCH_W), rows(COL_CG)),
        pl.BlockSpec((tc_in, LANES), rows(0)),
        pl.BlockSpec((CHUNK, D_HEAD), lambda n, c: (c, 0)),
        pl.BlockSpec((CHUNK, D_HEAD), lambda n, c: (c, 0)),
        pl.BlockSpec((CONV_W, A_QKV), const2),
        pl.BlockSpec((8, LANES), const2),
        pl.BlockSpec((1, D_HEAD), const2),
        pl.BlockSpec((1, D_HEAD), const2),
        pl.BlockSpec((N_HEADS, CHUNK, CHUNK), lambda n, c: (0, 0, 0)),
        pl.BlockSpec((CHUNK, LANES), const2),
    ]
    args = [p, p, p, p, p, p, ps, cos2, sin2, convw, avec, anorm, cnorm, dmat, rtab]
    if init is not None:
        in_specs += [state_spec, state_spec, pl.BlockSpec((1, 8, A_QKV), lambda n, c: (n, 0, 0))]
        args += list(init)
    out_specs = [
        pl.BlockSpec((tc_in, BRANCH_W), rows(0)),
        pl.BlockSpec((tc_in, BRANCH_W), rows(0)),
        pl.BlockSpec((tc_in, LANES), rows(0)),
        pl.BlockSpec((tc_in, LANES), rows(0)),
        state_spec,
        state_spec,
    ]
    out_shape = [
        jax.ShapeDtypeStruct((r, BRANCH_W), F32),
        jax.ShapeDtypeStruct((r, BRANCH_W), F32),
        jax.ShapeDtypeStruct((r, LANES), F32),
        jax.ShapeDtypeStruct((r, LANES), F32),
        jax.ShapeDtypeStruct((n_seq, N_HEADS, D_HEAD, D_HEAD), F32),
        jax.ShapeDtypeStruct((n_seq, N_HEADS, D_HEAD, D_HEAD), F32),
    ]
    kern = functools.partial(_seq_kernel, tc_in=tc_in, valid_hi=valid_hi, front_pad=front_pad,
                             has_init=init is not None, precise=precise)
    return pl.pallas_call(
        kern,
        grid=(n_seq, n_chunks),
        in_specs=in_specs,
        out_specs=out_specs,
        out_shape=out_shape,
        scratch_shapes=[pltpu.VMEM((CHUNK + 8, A_QKV), F32), pltpu.VMEM((CHUNK, A_QKV), F32),
                        pltpu.VMEM((1, LANES), F32)],
        compiler_params=_params(2),
        name=("seq_prompt" if init is None else "seq_sample") + ("_split" if precise else ""),
    )(*args)


def _fox_prompt_kernel(q_ref, k_ref, v_ref, ct_ref, o_ref, m_ref, l_ref, acc_ref, *, tb, precise):
    mm = functools.partial(_dot, precise=precise)
    i = pl.program_id(1)
    j = pl.program_id(2)

    @pl.when(j == 0)
    def _():
        m_ref[...] = jnp.full_like(m_ref, NEG)
        l_ref[...] = jnp.zeros_like(l_ref)
        acc_ref[...] = jnp.zeros_like(acc_ref)

    @pl.when(j <= i)
    def _():
        qpos = i * tb + lax.broadcasted_iota(jnp.int32, (tb, tb), 0)
        kpos = j * tb + lax.broadcasted_iota(jnp.int32, (tb, tb), 1)
        ok = (kpos <= qpos) & (kpos >= PAD_FRONT)
        for h in range(N_HEADS):
            hs = slice(h * D_HEAD, (h + 1) * D_HEAD)
            s = mm(q_ref[:, hs], k_ref[:, hs], NT) * (D_HEAD ** -0.5) - ct_ref[0, h:h + 1, :]
            s = jnp.where(ok, s, NEG)
            m_old = m_ref[h][:, 0:1]
            m_new = jnp.maximum(m_old, jnp.max(s, axis=-1, keepdims=True))
            alpha = jnp.exp(m_old - m_new)
            p = jnp.exp(s - m_new)
            l_ref[h] = alpha * l_ref[h] + jnp.sum(p, axis=-1, keepdims=True)
            acc_ref[h] = alpha * acc_ref[h] + mm(p, v_ref[:, hs])
            m_ref[h] = jnp.broadcast_to(m_new, (tb, LANES))

    @pl.when(j == i)
    def _():
        qvalid = (i * tb + lax.broadcasted_iota(jnp.int32, (tb, D_HEAD), 0)) >= PAD_FRONT
        for h in range(N_HEADS):
            o = acc_ref[h] / l_ref[h][:, 0:1]
            o_ref[:, h * D_HEAD:(h + 1) * D_HEAD] = jnp.where(qvalid, o, 0.0)


def _fox_prompt(p, ct, n_seq, lp, tb, precise):
    r = p.shape[0]
    nb = lp // tb
    return pl.pallas_call(
        functools.partial(_fox_prompt_kernel, tb=tb, precise=precise),
        grid=(n_seq, nb, nb),
        in_specs=[
            pl.BlockSpec((tb, BRANCH_W), lambda n, i, j: (n * nb + i, COL_BQ)),
            pl.BlockSpec((tb, BRANCH_W), lambda n, i, j: (n * nb + jnp.minimum(i, j), COL_BK)),
            pl.BlockSpec((tb, BRANCH_W), lambda n, i, j: (n * nb + jnp.minimum(i, j), COL_BV)),
            pl.BlockSpec((1, 8, tb), lambda n, i, j: (n, 0, jnp.minimum(i, j))),
        ],
        out_specs=pl.BlockSpec((tb, BRANCH_W), lambda n, i, j: (n * nb + i, 0)),
        out_shape=jax.ShapeDtypeStruct((r, BRANCH_W), F32),
        scratch_shapes=[pltpu.VMEM((N_HEADS, tb, LANES), F32), pltpu.VMEM((N_HEADS, tb, LANES), F32),
                        pltpu.VMEM((N_HEADS, tb, D_HEAD), F32)],
        compiler_params=_params(3),
        name="fox_prompt",
    )(p, p, p, ct)


def _page_cumsum_kernel(lf_ref, c_ref, tot_ref):
    n = CHUNK * N_HEADS
    ri = lax.broadcasted_iota(jnp.int32, (n, n), 0)
    ci = lax.broadcasted_iota(jnp.int32, (n, n), 1)
    same_head = (ri % N_HEADS) == (ci % N_HEADS)
    upper = jnp.where(same_head & (ri <= ci), 1.0, 0.0).astype(BF16)
    total = jnp.where(same_head, 1.0, 0.0).astype(BF16)
    parts = _split3(lf_ref[...])
    c_ref[...] = sum(jnp.dot(x, upper, preferred_element_type=F32) for x in parts)
    tot_ref[...] = sum(jnp.dot(x, total, preferred_element_type=F32) for x in parts)


def _page_cumsum(lf):
    n_pages, n = lf.shape
    tp = _pick(n_pages, (256, 128, 64, 32, 16, 8))
    spec = pl.BlockSpec((tp, n), lambda i: (i, 0))
    return pl.pallas_call(
        _page_cumsum_kernel,
        grid=(n_pages // tp,),
        in_specs=[spec],
        out_specs=[spec, spec],
        out_shape=[jax.ShapeDtypeStruct((n_pages, n), F32)] * 2,
        compiler_params=_params(1),
        name="page_cumsum",
    )(lf)


def _fox_sample_kernel(*refs, ppg, n_q, precise):
    q_ref, kn_ref, vn_ref, cn_ref = refs[1:5]
    k_refs = refs[5:5 + ppg]
    v_refs = refs[5 + ppg:5 + 2 * ppg]
    c_refs = refs[5 + 2 * ppg:5 + 3 * ppg]
    tot_refs = refs[5 + 3 * ppg:5 + 4 * ppg]
    o_ref, m_ref, l_ref, acc_ref, carry = refs[5 + 4 * ppg:]
    mm = functools.partial(_dot, precise=precise)
    g = pl.program_id(1)
    n_keys = CHUNK * N_HEADS

    @pl.when(g == 0)
    def _():
        m_ref[...] = jnp.full_like(m_ref, NEG)
        l_ref[...] = jnp.zeros_like(l_ref)
        acc_ref[...] = jnp.zeros_like(acc_ref)
        carry[...] = jnp.zeros_like(carry)

    row = lax.broadcasted_iota(jnp.int32, (n_q, n_keys), 0)
    lane = lax.broadcasted_iota(jnp.int32, (n_q, n_keys), 1)
    same_head = (row % N_HEADS) == (lane % N_HEADS)
    scale = D_HEAD ** -0.5
    q = _split2(q_ref[0])

    def update(s, v):
        m_old = m_ref[:, 0:1]
        m_new = jnp.maximum(m_old, jnp.max(s, axis=-1, keepdims=True))
        alpha = jnp.exp(m_old - m_new)
        p = jnp.exp(s - m_new)
        l_ref[...] = alpha * l_ref[...] + jnp.sum(p, axis=-1, keepdims=True)
        acc_ref[...] = alpha * acc_ref[...] + mm(p, v)
        m_ref[...] = jnp.broadcast_to(m_new, m_ref.shape)

    for pg in range(ppg):
        bias = carry[...] + c_refs[pg][0]
        s = mm(q, k_refs[pg][...], NT) * scale - bias
        update(jnp.where(same_head, s, NEG), v_refs[pg][...])
        carry[...] = carry[...] + tot_refs[pg][0]

    @pl.when(g == pl.num_programs(1) - 1)
    def _():
        zpad = jnp.zeros((CHUNK - n_q, D_HEAD), F32)
        kn = jnp.concatenate([kn_ref[0], zpad], axis=0)
        vn = jnp.concatenate([vn_ref[0], zpad], axis=0)
        bias = carry[:, 0:CHUNK] + cn_ref[0]
        s = mm(q, kn, NT) * scale - bias
        r = lax.broadcasted_iota(jnp.int32, (n_q, CHUNK), 0)
        c = lax.broadcasted_iota(jnp.int32, (n_q, CHUNK), 1)
        ok = ((r % N_HEADS) == (c % N_HEADS)) & (c // N_HEADS <= r // N_HEADS) & (c < n_q)
        update(jnp.where(ok, s, NEG), vn)
        o_ref[0] = acc_ref[...] / l_ref[:, 0:1]


def _fox_sample(page_table, q, kn, vn, cn, cache_k, cache_v, c_page, tot_page, *, layer, n_pool, ppg, precise):
    n_seq, n_q, _ = q.shape
    n_pages = page_table.shape[1]
    n_keys = CHUNK * N_HEADS
    base = layer * n_pool
    pt = page_table.reshape(-1)

    def page(pg, ndim):
        return lambda n, g, pt_ref: (base + pt_ref[n * n_pages + g * ppg + pg],) + (0,) * (ndim - 1)

    seq3 = lambda n, g, pt_ref: (n, 0, 0)
    in_specs = [pl.BlockSpec((1, n_q, D_HEAD), seq3)] * 3 + [pl.BlockSpec((1, 1, CHUNK), seq3)]
    in_specs += [pl.BlockSpec((n_keys, D_HEAD), page(pg, 2)) for _ in range(2) for pg in range(ppg)]
    in_specs += [pl.BlockSpec((1, 1, n_keys), page(pg, 3)) for _ in range(2) for pg in range(ppg)]
    grid_spec = pltpu.PrefetchScalarGridSpec(
        num_scalar_prefetch=1,
        grid=(n_seq, n_pages // ppg),
        in_specs=in_specs,
        out_specs=pl.BlockSpec((1, n_q, D_HEAD), seq3),
        scratch_shapes=[pltpu.VMEM((n_q, LANES), F32), pltpu.VMEM((n_q, LANES), F32),
                        pltpu.VMEM((n_q, D_HEAD), F32), pltpu.VMEM((1, n_keys), F32)],
    )
    return pl.pallas_call(
        functools.partial(_fox_sample_kernel, ppg=ppg, n_q=n_q, precise=precise),
        grid_spec=grid_spec,
        out_shape=jax.ShapeDtypeStruct((n_seq, n_q, D_HEAD), F32),
        compiler_params=_params(2),
        name="fox_sample_split" if precise else "fox_sample",
    )(pt, q, kn, vn, cn, *([cache_k] * ppg), *([cache_v] * ppg), *([c_page] * ppg), *([tot_page] * ppg))


def _merge_kernel(oa_ref, ob_ref, oc_ref, gates_ref, x_ref, wb_ref, wo_ref, nf_ref, wr_ref, br_ref,
                  x1_ref, xn_ref, gate_ref, *, precise):
    mm = functools.partial(_dot, precise=precise)
    tm = x_ref.shape[0]
    merged = None
    for b, o_ref in enumerate((oa_ref, ob_ref, oc_ref)):
        term = mm(o_ref[...], wb_ref[b]) * _sigmoid(gates_ref[:, b * D_MODEL:(b + 1) * D_MODEL])
        merged = term if merged is None else merged + term
    x1 = x_ref[...] + mm(merged, wo_ref[...])
    x1_ref[...] = x1
    xn = _split2(_rms(x1, nf_ref[...]))
    xn_ref[...] = xn[0]
    logits = _dot(xn, wr_ref[...], precise=True) + br_ref[...]

    lane = lax.broadcasted_iota(jnp.int32, (tm, LANES), 1).astype(F32)
    big = float(LANES)
    is_grp = lane < float(N_GROUPS)
    mx = jnp.max(jnp.where(is_grp, logits, NEG), axis=-1, keepdims=True)
    p_grp = 1.0 / jnp.sum(jnp.where(is_grp, jnp.exp(jnp.where(is_grp, logits - mx, 0.0)), 0.0),
                          axis=-1, keepdims=True)
    g_star = jnp.min(jnp.where(is_grp & (logits == mx), lane, big), axis=-1, keepdims=True)
    base = float(LANE_EXPERT) + float(EXP_PER_GROUP) * g_star
    is_exp = (lane >= base) & (lane < base + float(EXP_PER_GROUP))
    le = jnp.where(is_exp, logits, NEG)
    t1 = jnp.max(le, axis=-1, keepdims=True)
    i1 = jnp.min(jnp.where(is_exp & (le == t1), lane, big), axis=-1, keepdims=True)
    le2 = jnp.where(lane == i1, NEG, le)
    t2 = jnp.max(le2, axis=-1, keepdims=True)
    i2 = jnp.min(jnp.where(is_exp & (lane != i1) & (le2 == t2), lane, big), axis=-1, keepdims=True)
    e2 = jnp.exp(t2 - t1)
    w1 = p_grp / (1.0 + e2)
    w2 = w1 * e2
    gate_ref[...] = jnp.where(lane == i1, w1, jnp.where(lane == i2, w2, 0.0))


def _merge(oa, ob, oc, p, x, wb, wo, nf, wr, br, tm, precise):
    r = x.shape[0]
    rows = lambda i: (i, 0)
    const2 = lambda i: (0, 0)
    return pl.pallas_call(
        functools.partial(_merge_kernel, precise=precise),
        grid=(r // tm,),
        in_specs=[
            pl.BlockSpec((tm, BRANCH_W), rows),
            pl.BlockSpec((tm, BRANCH_W), rows),
            pl.BlockSpec((tm, BRANCH_W), rows),
            pl.BlockSpec((tm, N_BRANCH * D_MODEL), rows),
            pl.BlockSpec((tm, D_MODEL), rows),
            pl.BlockSpec((N_BRANCH, BRANCH_W, D_MODEL), lambda i: (0, 0, 0)),
            pl.BlockSpec((D_MODEL, D_MODEL), const2),
            pl.BlockSpec((1, D_MODEL), const2),
            pl.BlockSpec((D_MODEL, LANES), const2),
            pl.BlockSpec((1, LANES), const2),
        ],
        out_specs=[
            pl.BlockSpec((tm, D_MODEL), rows),
            pl.BlockSpec((tm, D_MODEL), rows),
            pl.BlockSpec((tm, LANES), rows),
        ],
        out_shape=[jax.ShapeDtypeStruct((r, D_MODEL), F32), jax.ShapeDtypeStruct((r, D_MODEL), BF16),
                   jax.ShapeDtypeStruct((r, LANES), F32)],
        compiler_params=_params(1),
        name="merge_split" if precise else "merge",
    )(oa, ob, oc, p, x, wb, wo, nf, wr, br)


def _moe_kernel(xn_ref, gate_ref, x1_ref, w1_ref, w3_ref, w2_ref, nfin_ref, *out_refs, final):
    x2_ref = out_refs[0]
    e = pl.program_id(1)

    @pl.when(e == 0)
    def _():
        x2_ref[...] = x1_ref[...]

    xn = xn_ref[...]
    lane = lax.broadcasted_iota(jnp.int32, gate_ref.shape, 1)
    g_e = jnp.sum(jnp.where(lane == LANE_EXPERT + e, gate_ref[...], 0.0), axis=-1, keepdims=True)
    h = _silu(_dot(xn, w1_ref[0])) * _dot(xn, w3_ref[0])
    x2_ref[...] += _dot(h * g_e, w2_ref[0])

    if final:
        @pl.when(e == pl.num_programs(1) - 1)
        def _():
            out_refs[1][...] = _rms(x2_ref[...], nfin_ref[...])


def _moe(xn, gate, x1, w1, w3, w2, nfin, tm, final):
    r = x1.shape[0]
    rows = lambda i, e: (i, 0)
    out_specs = [pl.BlockSpec((tm, D_MODEL), rows)]
    out_shape = [jax.ShapeDtypeStruct((r, D_MODEL), F32)]
    if final:
        out_specs.append(pl.BlockSpec((tm, D_MODEL), rows))
        out_shape.append(jax.ShapeDtypeStruct((r, D_MODEL), F32))
    return pl.pallas_call(
        functools.partial(_moe_kernel, final=final),
        grid=(r // tm, N_EXPERTS),
        in_specs=[
            pl.BlockSpec((tm, D_MODEL), rows),
            pl.BlockSpec((tm, LANES), rows),
            pl.BlockSpec((tm, D_MODEL), rows),
            pl.BlockSpec((1, D_MODEL, D_FF), lambda i, e: (e, 0, 0)),
            pl.BlockSpec((1, D_MODEL, D_FF), lambda i, e: (e, 0, 0)),
            pl.BlockSpec((1, D_FF, D_MODEL), lambda i, e: (e, 0, 0)),
            pl.BlockSpec((1, D_MODEL), lambda i, e: (0, 0)),
        ],
        out_specs=out_specs,
        out_shape=out_shape,
        compiler_params=_params(2),
        name="moe",
    )(xn, gate, x1, w1, w3, w2, nfin)


def _rope_tables(pos):
    half = D_HEAD // 2
    inv = jnp.asarray((ROPE_BASE ** (-np.arange(half, dtype=np.float64) / half)).astype(np.float32))
    ang = pos.astype(F32)[:, None] * inv[None, :]
    cos, sin = jnp.cos(ang), jnp.sin(ang)
    return jnp.concatenate([cos, cos], axis=-1), jnp.concatenate([-sin, sin], axis=-1)


def _retention_tables(t_eff):
    idx = jnp.arange(CHUNK, dtype=F32)
    log_gamma = jnp.log(1.0 - jnp.exp2(-5.0 - jnp.arange(N_HEADS, dtype=F32)))
    incl = idx[:, None] >= idx[None, :]
    rel = jnp.where(incl, idx[:, None] - idx[None, :], 0.0)
    dmat = jnp.where(incl, jnp.exp(rel[None] * log_gamma[:, None, None]), 0.0)
    cross = jnp.exp((idx + 1.0)[:, None] * log_gamma[None, :])
    kdec = jnp.exp((t_eff - 1.0 - idx)[:, None] * log_gamma[None, :])
    tot = jnp.broadcast_to(jnp.exp(t_eff * log_gamma)[None, :], (CHUNK, N_HEADS))
    rtab = jnp.concatenate([cross, kdec, tot, jnp.zeros((CHUNK, LANES - 3 * N_HEADS), F32)], axis=1)
    return dmat, rtab


def _lanes_row(pairs):
    row = jnp.zeros((LANES,), F32)
    for start, vals in pairs:
        row = row.at[start:start + vals.shape[0]].set(vals.astype(F32))
    return row


def _pick(n, candidates):
    for c in candidates:
        if n % c == 0:
            return c
    raise ValueError(f"no tile in {candidates} divides {n}")


def kernel(x_prompt, x_sample, cache_k, cache_v, cache_logf, page_table, state_delta, state_conv, state_ret,
           meta_tokens, norm_mix, norm_ffn, norm_final, w_in, conv_w, a_log, dt_bias, a_norm, b_fbias, c_norm,
           w_branch, w_out, w_router_group, b_router_group, w_router_expert, b_router_expert, w1, w3, w2):
    nb, seq, _ = x_prompt.shape
    ns, ts, _ = x_sample.shape
    depth, n_pool = cache_k.shape[:2]
    n_pages = page_table.shape[1]
    assert seq % CHUNK == 0 and ts <= SAMPLE_ROWS and cache_k.shape[2] == CHUNK
    lp = CHUNK + seq
    n_chunks = lp // CHUNK
    rp = nb * lp
    rs = ns * SAMPLE_ROWS
    tm_p = _pick(rp, (512, 256, 128))
    tm_s = _pick(rs, (256, 128, 64, 32, 16, 8))

    xp = jnp.concatenate([jnp.zeros((nb, PAD_FRONT, D_MODEL), F32),
                          jnp.broadcast_to(meta_tokens[None], (nb, N_META, D_MODEL)), x_prompt],
                         axis=1).reshape(rp, D_MODEL)
    xs = jnp.concatenate([x_sample, jnp.zeros((ns, SAMPLE_ROWS - ts, D_MODEL), F32)], axis=1).reshape(rs, D_MODEL)

    cos_p, sin_p = _rope_tables(jnp.arange(lp) - PAD_FRONT)
    cos_s, sin_s = _rope_tables(n_pages * CHUNK + jnp.arange(CHUNK))
    dmat_p, rtab_p = _retention_tables(float(CHUNK))
    dmat_s, rtab_s = _retention_tables(float(ts))

    assert (ts * N_HEADS) % 8 == 0
    n_cache = depth * n_pool
    cache_k2 = cache_k.reshape(n_cache * CHUNK * N_HEADS, D_HEAD)
    cache_v2 = cache_v.reshape(n_cache * CHUNK * N_HEADS, D_HEAD)
    c_page, tot_page = _page_cumsum(cache_logf.reshape(n_cache, CHUNK * N_HEADS))
    c_page = c_page.reshape(n_cache, 1, CHUNK * N_HEADS)
    tot_page = tot_page.reshape(n_cache, 1, CHUNK * N_HEADS)
    tb = _pick(lp, (384, 256, 128))
    ppg = _pick(n_pages, (8, 4, 2, 1))

    o_beta = A_QKV + BRANCH_W
    o_bq = o_beta + 8
    o_f = o_bq + 3 * BRANCH_W
    o_gates = o_f + 4 + 4 * BRANCH_W

    def sample_heads(a):
        return a.reshape(ns, SAMPLE_ROWS, N_HEADS, D_HEAD)[:, :ts].reshape(ns, ts * N_HEADS, D_HEAD)

    def prompt_rows(a):
        return a.reshape(nb, lp, a.shape[-1])[:, PAD_FRONT:]

    def sample_rows(a):
        return a.reshape(ns, SAMPLE_ROWS, a.shape[-1])[:, :ts]

    def col(a, c, w=BRANCH_W):
        return a[:, c * BRANCH_W:c * BRANCH_W + w]

    outs_p = [[] for _ in range(6)]
    outs_s = [[] for _ in range(6)]
    yp = ys = None
    for l in range(depth):
        final = l == depth - 1
        hp = not final
        wdt = F32 if hp else BF16
        wl = w_in[l]
        w_main = jnp.concatenate([wl[:, o_gates:], wl[:, :o_beta], wl[:, o_bq:o_f], wl[:, o_f + 4:o_gates]],
                                 axis=1).astype(wdt)
        w_small = jnp.concatenate([wl[:, o_beta:o_bq], wl[:, o_f:o_f + 4],
                                   jnp.zeros((D_MODEL, LANES - 12), F32)], axis=1).astype(wdt)
        g_mix = norm_mix[l][None]
        p_p, ps_p = _proj(xp, g_mix, w_main, w_small, tm_p, 512, hp)
        p_s, ps_s = _proj(xs, g_mix, w_main, w_small, tm_s, 512, hp)

        avec = jnp.zeros((8, LANES), F32)
        avec = avec.at[0].set(_lanes_row([(LANE_ALPHA, a_log[l])]))
        avec = avec.at[1].set(_lanes_row([(LANE_ALPHA, dt_bias[l])]))
        avec = avec.at[2].set(_lanes_row([(LANE_F, b_fbias[l])]))
        shared = (conv_w[l], avec, a_norm[l][None], c_norm[l][None])
        oa_p, oc_p, lf_p, cc_p, sd_p, sr_p = _seq(
            p_p, ps_p, cos_p, sin_p, *shared, dmat_p, rtab_p, n_seq=nb, n_chunks=n_chunks, tc_in=CHUNK,
            valid_hi=CHUNK, front_pad=PAD_FRONT, precise=hp)
        cb0 = jnp.pad(state_conv[l], ((0, 0), (8 - (CONV_W - 1), 0), (0, 0)))
        oa_s, oc_s, lf_s, cc_s, sd_s, sr_s = _seq(
            p_s, ps_s, cos_s, sin_s, *shared, dmat_s, rtab_s, n_seq=ns, n_chunks=1, tc_in=SAMPLE_ROWS,
            valid_hi=ts, front_pad=0, precise=hp, init=(state_delta[l], state_ret[l], cb0))

        ct = jnp.pad(jnp.swapaxes(cc_p[:, LANE_F:LANE_F + N_HEADS].reshape(nb, lp, N_HEADS), 1, 2),
                     ((0, 0), (0, 8 - N_HEADS), (0, 0)))
        ob_p = _fox_prompt(p_p, ct, nb, lp, tb, hp)
        cn = cc_s[:, LANE_F:LANE_F + N_HEADS].reshape(ns, SAMPLE_ROWS, N_HEADS)[:, :ts].reshape(ns, 1, ts * N_HEADS)
        cn = jnp.pad(cn, ((0, 0), (0, 0), (0, CHUNK - ts * N_HEADS)))
        o_s = _fox_sample(page_table, sample_heads(col(p_s, COL_BQ)), sample_heads(col(p_s, COL_BK)),
                          sample_heads(col(p_s, COL_BV)), cn, cache_k2, cache_v2, c_page, tot_page,
                          layer=l, n_pool=n_pool, ppg=ppg, precise=hp)
        ob_s = jnp.pad(o_s.reshape(ns, ts, BRANCH_W), ((0, 0), (0, SAMPLE_ROWS - ts), (0, 0))).reshape(rs, BRANCH_W)

        wr = jnp.concatenate([w_router_group[l], w_router_expert[l],
                              jnp.zeros((D_MODEL, LANES - N_GROUPS - N_EXPERTS), F32)], axis=1)
        br = _lanes_row([(LANE_GROUP, b_router_group[l]), (LANE_EXPERT, b_router_expert[l].reshape(-1))])[None]
        nf = norm_ffn[l][None]
        wb, wo = w_branch[l].astype(wdt), w_out[l].astype(wdt)
        x1_p, xn_p, gate_p = _merge(oa_p, ob_p, oc_p, p_p, xp, wb, wo, nf, wr, br, min(tm_p, 256), hp)
        x1_s, xn_s, gate_s = _merge(oa_s, ob_s, oc_s, p_s, xs, wb, wo, nf, wr, br, tm_s, hp)

        ew1 = w1[l].reshape(N_EXPERTS, D_MODEL, D_FF).astype(BF16)
        ew3 = w3[l].reshape(N_EXPERTS, D_MODEL, D_FF).astype(BF16)
        ew2 = w2[l].reshape(N_EXPERTS, D_FF, D_MODEL).astype(BF16)
        res_p = _moe(xn_p, gate_p, x1_p, ew1, ew3, ew2, norm_final[None], tm_p, final)
        res_s = _moe(xn_s, gate_s, x1_s, ew1, ew3, ew2, norm_final[None], tm_s, final)
        xp, xs = res_p[0], res_s[0]
        if final:
            yp, ys = res_p[1], res_s[1]

        for lst, rows_of, p, lf_a, sd, sr in ((outs_p, prompt_rows, p_p, lf_p, sd_p, sr_p),
                                              (outs_s, sample_rows, p_s, lf_s, sd_s, sr_s)):
            k_new = rows_of(col(p, COL_BK))
            v_new = rows_of(col(p, COL_BV))
            lst[0].append(k_new.reshape(k_new.shape[:2] + (N_HEADS, D_HEAD)))
            lst[1].append(v_new.reshape(v_new.shape[:2] + (N_HEADS, D_HEAD)))
            lst[2].append(rows_of(lf_a[:, LANE_F:LANE_F + N_HEADS]))
            lst[3].append(sd)
            lst[4].append(rows_of(col(p, COL_AQKV, A_QKV))[:, -(CONV_W - 1):])
            lst[5].append(sr)

    y_prompt = yp.reshape(nb, lp, D_MODEL)[:, CHUNK:]
    y_sample = ys.reshape(ns, SAMPLE_ROWS, D_MODEL)[:, :ts]
    return (y_prompt, y_sample, *(jnp.stack(a) for a in outs_p), *(jnp.stack(a) for a in outs_s))
```

```python
import functools

import jax
import jax.numpy as jnp
import numpy as np
from jax import lax
from jax.experimental import pallas as pl
from jax.experimental.pallas import tpu as pltpu

F32 = jnp.float32
BF16 = jnp.bfloat16

D_MODEL = 1024
N_META = 16
CHUNK = 128
PAD_FRONT = CHUNK - N_META
N_HEADS = 4
D_HEAD = 128
BRANCH_W = N_HEADS * D_HEAD
A_QKV = 3 * BRANCH_W
CONV_W = 4
N_BRANCH = 3
N_GROUPS = 4
EXP_PER_GROUP = 4
N_EXPERTS = N_GROUPS * EXP_PER_GROUP
D_FF = 512
ROPE_BASE = 10000.0
EPS = 1e-6
NEG = -1e30
SAMPLE_ROWS = 8
LANES = 128

COL_GATES = 0
COL_AQKV = 6
COL_AZ = 9
COL_BQ, COL_BK, COL_BV = 10, 11, 12
COL_CQ, COL_CK, COL_CV, COL_CG = 13, 14, 15, 16
N_MAIN = 17 * BRANCH_W
LANE_BETA, LANE_ALPHA, LANE_F = 0, 4, 8
LANE_GROUP, LANE_EXPERT = 0, 4

VMEM_LIMIT = 56 * 1024 * 1024
PAR = 2

NN = (((1,), (0,)), ((), ()))
NT = (((1,), (1,)), ((), ()))
TN = (((0,), (0,)), ((), ()))


def _split2(x):
    hi = x.astype(BF16)
    return hi, (x - hi.astype(F32)).astype(BF16)


def _dot(a, b, dims=NN, precise=False):
    def dg(x, y):
        return lax.dot_general(x, y, dims, preferred_element_type=F32)

    if not precise:
        a = a[0] if isinstance(a, tuple) else a.astype(BF16)
        b = b[0] if isinstance(b, tuple) else b.astype(BF16)
        return dg(a, b)
    a_hi, a_lo = a if isinstance(a, tuple) else _split2(a)
    b_hi, b_lo = b if isinstance(b, tuple) else _split2(b)
    return dg(a_hi, b_hi) + dg(a_hi, b_lo) + dg(a_lo, b_hi)


def _split3(x):
    x1 = x.astype(BF16)
    r1 = x - x1.astype(F32)
    x2 = r1.astype(BF16)
    x3 = (r1 - x2.astype(F32)).astype(BF16)
    return x1, x2, x3


def _sigmoid(x):
    return 1.0 / (1.0 + jnp.exp(-x))


def _silu(x):
    return x * _sigmoid(x)


def _softplus(x):
    return jnp.maximum(x, 0.0) + jnp.log1p(jnp.exp(-jnp.abs(x)))


def _rms(x, g):
    return x * lax.rsqrt(jnp.mean(x * x, axis=-1, keepdims=True) + EPS) * g


def _params(n_axes):
    return pltpu.CompilerParams(dimension_semantics=("arbitrary",) * n_axes, vmem_limit_bytes=VMEM_LIMIT)


def _proj_kernel(x_ref, g_ref, w_ref, ws_ref, p_ref, ps_ref, xn_ref, *, tn, precise):
    j = pl.program_id(1)

    @pl.when(j == 0)
    def _():
        xn = _split2(_rms(x_ref[...], g_ref[...]))
        xn_ref[0] = xn[0]
        if precise:
            xn_ref[1] = xn[1]
        ps_ref[...] = _dot(xn, ws_ref[...], precise=precise)

    if precise:
        p_ref[...] = _dot((xn_ref[0], xn_ref[1]), w_ref[...], precise=True)
    else:
        p_ref[...] = _dot(xn_ref[0], w_ref[:, pl.ds(pl.multiple_of(j * tn, tn), tn)])


def _proj(x, g, w_main, w_small, tm, tn, precise):
    r = x.shape[0]
    w_spec = (pl.BlockSpec((D_MODEL, tn), lambda i, j: (0, j)) if precise else
              pl.BlockSpec((D_MODEL, N_MAIN), lambda i, j: (0, 0)))
    return pl.pallas_call(
        functools.partial(_proj_kernel, tn=tn, precise=precise),
        grid=(r // tm, N_MAIN // tn),
        in_specs=[pl.BlockSpec((tm, D_MODEL), lambda i, j: (i, 0)),
                  pl.BlockSpec((1, D_MODEL), lambda i, j: (0, 0)),
                  w_spec,
                  pl.BlockSpec((D_MODEL, LANES), lambda i, j: (0, 0))],
        out_specs=[pl.BlockSpec((tm, tn), lambda i, j: (i, j)),
                   pl.BlockSpec((tm, LANES), lambda i, j: (i, 0))],
        out_shape=[jax.ShapeDtypeStruct((r, N_MAIN), F32), jax.ShapeDtypeStruct((r, LANES), F32)],
        scratch_shapes=[pltpu.VMEM((2 if precise else 1, tm, D_MODEL), BF16)],
        compiler_params=_params(2),
        name="proj_split" if precise else "proj",
    )(x, g, w_main, w_small)


def _delta_heads(qs, ks, vs, betas, gcols, grows, states, incl, strict, precise):
    mm = functools.partial(_dot, precise=precise)
    n = range(len(qs))
    egs = [jnp.exp(gcols[i]) for i in n]
    decays = [jnp.where(incl, jnp.exp(jnp.where(incl, gcols[i] - grows[i], 0.0)), 0.0) for i in n]
    kbs = [ks[i] * betas[i] for i in n]
    ms = [jnp.where(strict, mm(kbs[i], ks[i], NT) * decays[i], 0.0) for i in n]
    rs = [-m for m in ms]
    pows = [_dot(r, r) for r in rs]
    for it in range(6):
        rps = [_dot(rs[i], pows[i]) for i in n]
        if it < 5:
            nxt = [_dot(pows[i], pows[i]) for i in n]
        rs = [rs[i] + pows[i] + rps[i] for i in n]
        if it < 5:
            pows = nxt
    for _ in range(2 if precise else 1):
        es = [-(ms[i] + rs[i] + _dot(ms[i], rs[i], precise=True)) for i in n]
        res = [_dot(rs[i], es[i]) for i in n]
        rs = [rs[i] + es[i] + res[i] for i in n]
    vbs = [vs[i] * betas[i] for i in n]
    kbes = [kbs[i] * egs[i] for i in n]
    us = [vbs[i] + mm(rs[i], vbs[i]) for i in n]
    ws = [kbes[i] + mm(rs[i], kbes[i]) for i in n]
    attns = [mm(qs[i], ks[i], NT) * decays[i] for i in n]
    v_news = [us[i] - mm(ws[i], states[i]) for i in n]
    outs = [mm(qs[i] * egs[i], states[i]) + mm(attns[i], v_news[i]) for i in n]
    g_lasts = [g[CHUNK - 1:CHUNK, :] for g in gcols]
    new_states = [states[i] * jnp.exp(g_lasts[i]) + mm(ks[i] * jnp.exp(g_lasts[i] - gcols[i]), v_news[i], TN)
                  for i in n]
    return outs, new_states


def _seq_kernel(*refs, n_par, tc_in, valid_hi, front_pad, has_init, precise):
    if has_init:
        (aqkv_ref, az_ref, cq_ref, ck_ref, cv_ref, cg_ref, sm_ref, cos_ref, sin_ref,
         convw_ref, avec_ref, anorm_ref, cnorm_ref, dmat_ref, rtab_ref,
         sd0_ref, sr0_ref, cb0_ref,
         oa_ref, oc_ref, lf_ref, cc_ref, sd_ref, sr_ref, xbuf, cbuf, carry) = refs
    else:
        (aqkv_ref, az_ref, cq_ref, ck_ref, cv_ref, cg_ref, sm_ref, cos_ref, sin_ref,
         convw_ref, avec_ref, anorm_ref, cnorm_ref, dmat_ref, rtab_ref,
         oa_ref, oc_ref, lf_ref, cc_ref, sd_ref, sr_ref, xbuf, cbuf, carry) = refs
    mm = functools.partial(_dot, precise=precise)
    c = pl.program_id(1)

    def ext(ref, s, cols=None):
        v = ref[s] if cols is None else ref[s, :, cols]
        if tc_in == CHUNK:
            return v
        return jnp.concatenate([v, jnp.zeros((CHUNK - tc_in, v.shape[1]), v.dtype)], axis=0)

    @pl.when(c == 0)
    def _():
        carry[...] = jnp.zeros_like(carry)
        if has_init:
            xbuf[:, 0:8, :] = cb0_ref[...]
            sd_ref[...] = sd0_ref[...]
            sr_ref[...] = sr0_ref[...]
        else:
            xbuf[:, 0:8, :] = jnp.zeros((n_par, 8, A_QKV), F32)
            sd_ref[...] = jnp.zeros_like(sd_ref)
            sr_ref[...] = jnp.zeros_like(sr_ref)

    rows = lax.broadcasted_iota(jnp.int32, (CHUNK, LANES), 0)
    lane = lax.broadcasted_iota(jnp.int32, (CHUNK, LANES), 1)
    lo = jnp.where(c == 0, front_pad, 0)
    valid = (rows >= lo) & (rows < valid_hi)
    incl = rows >= lane
    strict = rows > lane
    is_g = (lane >= LANE_ALPHA) & (lane < LANE_ALPHA + N_HEADS)
    is_f = (lane >= LANE_F) & (lane < LANE_F + N_HEADS)
    tril = jnp.where(incl, 1.0, 0.0).astype(BF16)
    av = avec_ref[...]
    cos2 = cos_ref[...]
    sin2 = sin_ref[...]
    rtab = rtab_ref[...]

    heads = range(N_HEADS)
    hsl = [slice(h * D_HEAD, (h + 1) * D_HEAD) for h in heads]
    chains = [(s, h) for s in range(n_par) for h in heads]
    qs, ks, vs, betas, gcols, grows = [], [], [], [], [], []
    for s in range(n_par):
        for j in range(A_QKV // LANES):
            cols = slice(j * LANES, (j + 1) * LANES)
            xbuf[s, 8:8 + CHUNK, cols] = ext(aqkv_ref, s, cols)
            acc = xbuf[s, 5:5 + CHUNK, cols] * convw_ref[0:1, cols]
            for i in range(1, CONV_W):
                acc = acc + xbuf[s, 5 + i:5 + i + CHUNK, cols] * convw_ref[i:i + 1, cols]
            cbuf[s, :, cols] = jnp.where(valid, _silu(acc), 0.0)
            xbuf[s, 0:8, cols] = xbuf[s, CHUNK:CHUNK + 8, cols]

        sm = ext(sm_ref, s)
        g_all = -jnp.exp(av[0:1, :]) * _softplus(sm + av[1:2, :])
        lf_all = -_softplus(-(sm + av[2:3, :]))
        z = jnp.where(valid & is_g, g_all, jnp.where(valid & is_f, lf_all, 0.0))
        cum = sum(jnp.dot(tril, part, preferred_element_type=F32) for part in _split3(z))
        cum_t = cum.T
        cc = cum + carry[s]
        carry[s] = cc[CHUNK - 1:CHUNK, :]
        lf_ref[s] = z[:tc_in]
        cc_ref[s] = cc[:tc_in]

        beta_all = _sigmoid(sm)
        for h in heads:
            q = cbuf[s, :, hsl[h]]
            k = cbuf[s, :, BRANCH_W + h * D_HEAD:BRANCH_W + (h + 1) * D_HEAD]
            qs.append(q * lax.rsqrt(jnp.sum(q * q, axis=-1, keepdims=True) + EPS) * (D_HEAD ** -0.5))
            ks.append(k * lax.rsqrt(jnp.sum(k * k, axis=-1, keepdims=True) + EPS))
            vs.append(cbuf[s, :, 2 * BRANCH_W + h * D_HEAD:2 * BRANCH_W + (h + 1) * D_HEAD])
            betas.append(beta_all[:, LANE_BETA + h:LANE_BETA + h + 1])
            gcols.append(cum[:, LANE_ALPHA + h:LANE_ALPHA + h + 1])
            grows.append(cum_t[LANE_ALPHA + h:LANE_ALPHA + h + 1, :])

    outs, new_states = _delta_heads(qs, ks, vs, betas, gcols, grows, [sd_ref[s, h] for s, h in chains],
                                    incl, strict, precise)
    for i, (s, h) in enumerate(chains):
        sd_ref[s, h] = new_states[i]
        oa = _rms(outs[i], anorm_ref[...]) * _silu(ext(az_ref, s, hsl[h]))
        oa_ref[s, :, hsl[h]] = oa[:tc_in]

    qcs = [ext(cq_ref, s, hsl[h]) for s, h in chains]
    kcs = [ext(ck_ref, s, hsl[h]) for s, h in chains]
    vcs = [ext(cv_ref, s, hsl[h]) for s, h in chains]
    qcs = [q * cos2 + pltpu.roll(q, D_HEAD // 2, 1) * sin2 for q in qcs]
    kcs = [(k * cos2 + pltpu.roll(k, D_HEAD // 2, 1) * sin2) * (D_HEAD ** -0.5) for k in kcs]
    srs = [sr_ref[s, h] for s, h in chains]
    scores = [mm(qcs[i], kcs[i], NT) * dmat_ref[h] for i, (s, h) in enumerate(chains)]
    crosses = [mm(qcs[i] * rtab[:, h:h + 1], srs[i]) for i, (s, h) in enumerate(chains)]
    kvs = [mm(kcs[i] * rtab[:, 4 + h:5 + h], vcs[i], TN) for i, (s, h) in enumerate(chains)]
    inners = [mm(scores[i], vcs[i]) for i in range(len(chains))]
    for i, (s, h) in enumerate(chains):
        sr_ref[s, h] = srs[i] * rtab[0:1, 8 + h:9 + h] + kvs[i]
        oc = _rms(inners[i] + crosses[i], cnorm_ref[...]) * _silu(ext(cg_ref, s, hsl[h]))
        oc_ref[s, :, hsl[h]] = oc[:tc_in]


def _seq(p, ps, cos2, sin2, convw, avec, anorm, cnorm, dmat, rtab, *, n_seq, n_chunks, tc_in,
         valid_hi, front_pad, precise, n_par, init=None):
    r = p.shape[0]
    lp = n_chunks * tc_in
    p3 = p.reshape(n_seq, lp, N_MAIN)
    ps3 = ps.reshape(n_seq, lp, LANES)

    def rows(width, col):
        return pl.BlockSpec((n_par, tc_in, width), lambda n, c: (n, c, col))

    const2 = lambda n, c: (0, 0)
    state_spec = pl.BlockSpec((n_par, N_HEADS, D_HEAD, D_HEAD), lambda n, c: (n, 0, 0, 0))
    in_specs = [
        rows(A_QKV, COL_AQKV // 3),
        rows(BRANCH_W, COL_AZ),
        rows(BRANCH_W, COL_CQ),
        rows(BRANCH_W, COL_CK),
        rows(BRANCH_W, COL_CV),
        rows(BRANCH_W, COL_CG),
        rows(LANES, 0),
        pl.BlockSpec((CHUNK, D_HEAD), lambda n, c: (c, 0)),
        pl.BlockSpec((CHUNK, D_HEAD), lambda n, c: (c, 0)),
        pl.BlockSpec((CONV_W, A_QKV), const2),
        pl.BlockSpec((8, LANES), const2),
        pl.BlockSpec((1, D_HEAD), const2),
        pl.BlockSpec((1, D_HEAD), const2),
        pl.BlockSpec((N_HEADS, CHUNK, CHUNK), lambda n, c: (0, 0, 0)),
        pl.BlockSpec((CHUNK, LANES), const2),
    ]
    args = [p3, p3, p3, p3, p3, p3, ps3, cos2, sin2, convw, avec, anorm, cnorm, dmat, rtab]
    if init is not None:
        in_specs += [state_spec, state_spec, pl.BlockSpec((n_par, 8, A_QKV), lambda n, c: (n, 0, 0))]
        args += list(init)
    out_specs = [rows(BRANCH_W, 0), rows(BRANCH_W, 0), rows(LANES, 0), rows(LANES, 0), state_spec, state_spec]
    out_shape = [
        jax.ShapeDtypeStruct((n_seq, lp, BRANCH_W), F32),
        jax.ShapeDtypeStruct((n_seq, lp, BRANCH_W), F32),
        jax.ShapeDtypeStruct((n_seq, lp, LANES), F32),
        jax.ShapeDtypeStruct((n_seq, lp, LANES), F32),
        jax.ShapeDtypeStruct((n_seq, N_HEADS, D_HEAD, D_HEAD), F32),
        jax.ShapeDtypeStruct((n_seq, N_HEADS, D_HEAD, D_HEAD), F32),
    ]
    kern = functools.partial(_seq_kernel, n_par=n_par, tc_in=tc_in, valid_hi=valid_hi, front_pad=front_pad,
                             has_init=init is not None, precise=precise)
    oa, oc, lf, cc, sd, sr = pl.pallas_call(
        kern,
        grid=(n_seq // n_par, n_chunks),
        in_specs=in_specs,
        out_specs=out_specs,
        out_shape=out_shape,
        scratch_shapes=[pltpu.VMEM((n_par, CHUNK + 8, A_QKV), F32), pltpu.VMEM((n_par, CHUNK, A_QKV), F32),
                        pltpu.VMEM((n_par, 1, LANES), F32)],
        compiler_params=_params(2),
        name=("seq_prompt" if init is None else "seq_sample") + ("_split" if precise else ""),
    )(*args)
    return (oa.reshape(r, BRANCH_W), oc.reshape(r, BRANCH_W), lf.reshape(r, LANES), cc.reshape(r, LANES), sd, sr)


def _fox_prompt_kernel(q_ref, k_ref, v_ref, ct_ref, o_ref, m_ref, l_ref, acc_ref, *, tb, precise):
    mm = functools.partial(_dot, precise=precise)
    i = pl.program_id(1)
    j = pl.program_id(2)

    @pl.when(j == 0)
    def _():
        m_ref[...] = jnp.full_like(m_ref, NEG)
        l_ref[...] = jnp.zeros_like(l_ref)
        acc_ref[...] = jnp.zeros_like(acc_ref)

    def block(masked):
        if masked:
            qpos = i * tb + lax.broadcasted_iota(jnp.int32, (tb, tb), 0)
            kpos = j * tb + lax.broadcasted_iota(jnp.int32, (tb, tb), 1)
            ok = (kpos <= qpos) & (kpos >= PAD_FRONT)
        for h in range(N_HEADS):
            hs = slice(h * D_HEAD, (h + 1) * D_HEAD)
            s = mm(q_ref[:, hs] * (D_HEAD ** -0.5), k_ref[:, hs], NT) - ct_ref[0, h:h + 1, :]
            if masked:
                s = jnp.where(ok, s, NEG)
            m_old = m_ref[h][:, 0:1]
            m_new = jnp.maximum(m_old, jnp.max(s, axis=-1, keepdims=True))
            alpha = jnp.exp(m_old - m_new)
            p = jnp.exp(s - m_new)
            l_ref[h] = alpha * l_ref[h] + jnp.sum(p, axis=-1, keepdims=True)
            acc_ref[h] = alpha * acc_ref[h] + mm(p, v_ref[:, hs])
            m_ref[h] = jnp.broadcast_to(m_new, (tb, LANES))

    needs_mask = (j == i) | (j == 0)
    pl.when(needs_mask)(functools.partial(block, True))
    pl.when((j < i) & (j > 0))(functools.partial(block, False))

    @pl.when(j == i)
    def _():
        qvalid = (i * tb + lax.broadcasted_iota(jnp.int32, (tb, D_HEAD), 0)) >= PAD_FRONT
        for h in range(N_HEADS):
            o = acc_ref[h] / l_ref[h][:, 0:1]
            o_ref[:, h * D_HEAD:(h + 1) * D_HEAD] = jnp.where(qvalid, o, 0.0)


def _fox_prompt(p, ct, n_seq, lp, tb, precise):
    r = p.shape[0]
    nb = lp // tb
    return pl.pallas_call(
        functools.partial(_fox_prompt_kernel, tb=tb, precise=precise),
        grid=(n_seq, nb, nb),
        in_specs=[
            pl.BlockSpec((tb, BRANCH_W), lambda n, i, j: (n * nb + i, COL_BQ)),
            pl.BlockSpec((tb, BRANCH_W), lambda n, i, j: (n * nb + jnp.minimum(i, j), COL_BK)),
            pl.BlockSpec((tb, BRANCH_W), lambda n, i, j: (n * nb + jnp.minimum(i, j), COL_BV)),
            pl.BlockSpec((1, 8, tb), lambda n, i, j: (n, 0, jnp.minimum(i, j))),
        ],
        out_specs=pl.BlockSpec((tb, BRANCH_W), lambda n, i, j: (n * nb + i, 0)),
        out_shape=jax.ShapeDtypeStruct((r, BRANCH_W), F32),
        scratch_shapes=[pltpu.VMEM((N_HEADS, tb, LANES), F32), pltpu.VMEM((N_HEADS, tb, LANES), F32),
                        pltpu.VMEM((N_HEADS, tb, D_HEAD), F32)],
        compiler_params=_params(3),
        name="fox_prompt",
    )(p, p, p, ct)


def _page_cumsum_kernel(lf_ref, c_ref, tot_ref):
    n = CHUNK * N_HEADS
    ri = lax.broadcasted_iota(jnp.int32, (n, n), 0)
    ci = lax.broadcasted_iota(jnp.int32, (n, n), 1)
    same_head = (ri % N_HEADS) == (ci % N_HEADS)
    upper = jnp.where(same_head & (ri <= ci), 1.0, 0.0).astype(BF16)
    total = jnp.where(same_head, 1.0, 0.0).astype(BF16)
    parts = _split3(lf_ref[...])
    c_ref[...] = sum(jnp.dot(x, upper, preferred_element_type=F32) for x in parts)
    tot_ref[...] = sum(jnp.dot(x, total, preferred_element_type=F32) for x in parts)


def _page_cumsum(lf):
    n_pages, n = lf.shape
    tp = _pick(n_pages, (256, 128, 64, 32, 16, 8, n_pages))
    spec = pl.BlockSpec((tp, n), lambda i: (i, 0))
    return pl.pallas_call(
        _page_cumsum_kernel,
        grid=(n_pages // tp,),
        in_specs=[spec],
        out_specs=[spec, spec],
        out_shape=[jax.ShapeDtypeStruct((n_pages, n), F32)] * 2,
        compiler_params=_params(1),
        name="page_cumsum",
    )(lf)


def _fox_sample_kernel(*refs, ppg, n_q, precise):
    q_ref, kn_ref, vn_ref, cn_ref = refs[1:5]
    k_refs = refs[5:5 + ppg]
    v_refs = refs[5 + ppg:5 + 2 * ppg]
    c_refs = refs[5 + 2 * ppg:5 + 3 * ppg]
    tot_refs = refs[5 + 3 * ppg:5 + 4 * ppg]
    o_ref, m_ref, l_ref, acc_ref, carry = refs[5 + 4 * ppg:]
    mm = functools.partial(_dot, precise=precise)
    g = pl.program_id(1)
    n_keys = CHUNK * N_HEADS

    @pl.when(g == 0)
    def _():
        m_ref[...] = jnp.full_like(m_ref, NEG)
        l_ref[...] = jnp.zeros_like(l_ref)
        acc_ref[...] = jnp.zeros_like(acc_ref)
        carry[...] = jnp.zeros_like(carry)

    row = lax.broadcasted_iota(jnp.int32, (n_q, n_keys), 0)
    lane = lax.broadcasted_iota(jnp.int32, (n_q, n_keys), 1)
    same_head = (row % N_HEADS) == (lane % N_HEADS)
    scale = D_HEAD ** -0.5
    q = _split2(q_ref[0])

    def update(scores, values):
        top = scores[0]
        for s in scores[1:]:
            top = jnp.maximum(top, s)
        m_old = m_ref[:, 0:1]
        m_new = jnp.maximum(m_old, jnp.max(top, axis=-1, keepdims=True))
        alpha = jnp.exp(m_old - m_new)
        probs = [jnp.exp(s - m_new) for s in scores]
        mass = probs[0]
        for p in probs[1:]:
            mass = mass + p
        pv = mm(probs[0], values[0])
        for p, v in zip(probs[1:], values[1:]):
            pv = pv + mm(p, v)
        l_ref[...] = alpha * l_ref[...] + jnp.sum(mass, axis=-1, keepdims=True)
        acc_ref[...] = alpha * acc_ref[...] + pv
        m_ref[...] = jnp.broadcast_to(m_new, m_ref.shape)

    offset = carry[...]
    scores = []
    for pg in range(ppg):
        s = mm(q, k_refs[pg][...], NT) * scale - (offset + c_refs[pg][0])
        scores.append(jnp.where(same_head, s, NEG))
        offset = offset + tot_refs[pg][0]
    carry[...] = offset
    update(scores, [v_refs[pg][...] for pg in range(ppg)])

    @pl.when(g == pl.num_programs(1) - 1)
    def _():
        zpad = jnp.zeros((CHUNK - n_q, D_HEAD), F32)
        kn = jnp.concatenate([kn_ref[0], zpad], axis=0)
        vn = jnp.concatenate([vn_ref[0], zpad], axis=0)
        bias = carry[:, 0:CHUNK] + cn_ref[0]
        s = mm(q, kn, NT) * scale - bias
        r = lax.broadcasted_iota(jnp.int32, (n_q, CHUNK), 0)
        c = lax.broadcasted_iota(jnp.int32, (n_q, CHUNK), 1)
        ok = ((r % N_HEADS) == (c % N_HEADS)) & (c // N_HEADS <= r // N_HEADS) & (c < n_q)
        update([jnp.where(ok, s, NEG)], [vn])
        o_ref[0] = acc_ref[...] / l_ref[:, 0:1]


def _fox_sample(page_table, q, kn, vn, cn, cache_k, cache_v, c_page, tot_page, *, layer, n_pool, ppg, precise):
    n_seq, n_q, _ = q.shape
    n_pages = page_table.shape[1]
    n_keys = CHUNK * N_HEADS
    base = layer * n_pool
    pt = page_table.reshape(-1)

    def page(pg, ndim):
        return lambda n, g, pt_ref: (base + pt_ref[n * n_pages + g * ppg + pg],) + (0,) * (ndim - 1)

    seq3 = lambda n, g, pt_ref: (n, 0, 0)
    in_specs = [pl.BlockSpec((1, n_q, D_HEAD), seq3)] * 3 + [pl.BlockSpec((1, 1, CHUNK), seq3)]
    in_specs += [pl.BlockSpec((n_keys, D_HEAD), page(pg, 2)) for _ in range(2) for pg in range(ppg)]
    in_specs += [pl.BlockSpec((1, 1, n_keys), page(pg, 3)) for _ in range(2) for pg in range(ppg)]
    grid_spec = pltpu.PrefetchScalarGridSpec(
        num_scalar_prefetch=1,
        grid=(n_seq, n_pages // ppg),
        in_specs=in_specs,
        out_specs=pl.BlockSpec((1, n_q, D_HEAD), seq3),
        scratch_shapes=[pltpu.VMEM((n_q, LANES), F32), pltpu.VMEM((n_q, LANES), F32),
                        pltpu.VMEM((n_q, D_HEAD), F32), pltpu.VMEM((1, n_keys), F32)],
    )
    return pl.pallas_call(
        functools.partial(_fox_sample_kernel, ppg=ppg, n_q=n_q, precise=precise),
        grid_spec=grid_spec,
        out_shape=jax.ShapeDtypeStruct((n_seq, n_q, D_HEAD), F32),
        compiler_params=_params(2),
        name="fox_sample_split" if precise else "fox_sample",
    )(pt, q, kn, vn, cn, *([cache_k] * ppg), *([cache_v] * ppg), *([c_page] * ppg), *([tot_page] * ppg))


def _merge_kernel(oa_ref, ob_ref, oc_ref, gates_ref, x_ref, wb_ref, wo_ref, nf_ref, wr_ref, br_ref,
                  x1_ref, xn_ref, gate_ref, *, precise):
    mm = functools.partial(_dot, precise=precise)
    tm = x_ref.shape[0]
    merged = None
    for b, o_ref in enumerate((oa_ref, ob_ref, oc_ref)):
        term = mm(o_ref[...], wb_ref[b]) * _sigmoid(gates_ref[:, b * D_MODEL:(b + 1) * D_MODEL])
        merged = term if merged is None else merged + term
    x1 = x_ref[...] + mm(merged, wo_ref[...])
    x1_ref[...] = x1
    xn = _split2(_rms(x1, nf_ref[...]))
    xn_ref[...] = xn[0]
    logits = _dot(xn, wr_ref[...], precise=True) + br_ref[...]

    lane = lax.broadcasted_iota(jnp.int32, (tm, LANES), 1).astype(F32)
    big = float(LANES)
    is_grp = lane < float(N_GROUPS)
    mx = jnp.max(jnp.where(is_grp, logits, NEG), axis=-1, keepdims=True)
    p_grp = 1.0 / jnp.sum(jnp.where(is_grp, jnp.exp(jnp.where(is_grp, logits - mx, 0.0)), 0.0),
                          axis=-1, keepdims=True)
    g_star = jnp.min(jnp.where(is_grp & (logits == mx), lane, big), axis=-1, keepdims=True)
    base = float(LANE_EXPERT) + float(EXP_PER_GROUP) * g_star
    is_exp = (lane >= base) & (lane < base + float(EXP_PER_GROUP))
    le = jnp.where(is_exp, logits, NEG)
    t1 = jnp.max(le, axis=-1, keepdims=True)
    i1 = jnp.min(jnp.where(is_exp & (le == t1), lane, big), axis=-1, keepdims=True)
    le2 = jnp.where(lane == i1, NEG, le)
    t2 = jnp.max(le2, axis=-1, keepdims=True)
    i2 = jnp.min(jnp.where(is_exp & (lane != i1) & (le2 == t2), lane, big), axis=-1, keepdims=True)
    e2 = jnp.exp(t2 - t1)
    w1 = p_grp / (1.0 + e2)
    w2 = w1 * e2
    gate_ref[...] = jnp.where(lane == i1, w1, jnp.where(lane == i2, w2, 0.0))


def _merge(oa, ob, oc, p, x, wb, wo, nf, wr, br, tm, precise):
    r = x.shape[0]
    rows = lambda i: (i, 0)
    const2 = lambda i: (0, 0)
    return pl.pallas_call(
        functools.partial(_merge_kernel, precise=precise),
        grid=(r // tm,),
        in_specs=[
            pl.BlockSpec((tm, BRANCH_W), rows),
            pl.BlockSpec((tm, BRANCH_W), rows),
            pl.BlockSpec((tm, BRANCH_W), rows),
            pl.BlockSpec((tm, N_BRANCH * D_MODEL), rows),
            pl.BlockSpec((tm, D_MODEL), rows),
            pl.BlockSpec((N_BRANCH, BRANCH_W, D_MODEL), lambda i: (0, 0, 0)),
            pl.BlockSpec((D_MODEL, D_MODEL), const2),
            pl.BlockSpec((1, D_MODEL), const2),
            pl.BlockSpec((D_MODEL, LANES), const2),
            pl.BlockSpec((1, LANES), const2),
        ],
        out_specs=[
            pl.BlockSpec((tm, D_MODEL), rows),
            pl.BlockSpec((tm, D_MODEL), rows),
            pl.BlockSpec((tm, LANES), rows),
        ],
        out_shape=[jax.ShapeDtypeStruct((r, D_MODEL), F32), jax.ShapeDtypeStruct((r, D_MODEL), BF16),
                   jax.ShapeDtypeStruct((r, LANES), F32)],
        compiler_params=_params(1),
        name="merge_split" if precise else "merge",
    )(oa, ob, oc, p, x, wb, wo, nf, wr, br)


def _moe_kernel(xn_ref, gate_ref, x1_ref, w1_ref, w3_ref, w2_ref, nfin_ref, *out_refs, final):
    x2_ref = out_refs[0]
    e = pl.program_id(1)

    @pl.when(e == 0)
    def _():
        x2_ref[...] = x1_ref[...]

    xn = xn_ref[...]
    lane = lax.broadcasted_iota(jnp.int32, gate_ref.shape, 1)
    g_e = jnp.sum(jnp.where(lane == LANE_EXPERT + e, gate_ref[...], 0.0), axis=-1, keepdims=True)
    h = _silu(_dot(xn, w1_ref[0])) * _dot(xn, w3_ref[0])
    x2_ref[...] += _dot(h * g_e, w2_ref[0])

    if final:
        @pl.when(e == pl.num_programs(1) - 1)
        def _():
            out_refs[1][...] = _rms(x2_ref[...], nfin_ref[...])


def _moe(xn, gate, x1, w1, w3, w2, nfin, tm, final):
    r = x1.shape[0]
    rows = lambda i, e: (i, 0)
    out_specs = [pl.BlockSpec((tm, D_MODEL), rows)]
    out_shape = [jax.ShapeDtypeStruct((r, D_MODEL), F32)]
    if final:
        out_specs.append(pl.BlockSpec((tm, D_MODEL), rows))
        out_shape.append(jax.ShapeDtypeStruct((r, D_MODEL), F32))
    return pl.pallas_call(
        functools.partial(_moe_kernel, final=final),
        grid=(r // tm, N_EXPERTS),
        in_specs=[
            pl.BlockSpec((tm, D_MODEL), rows),
            pl.BlockSpec((tm, LANES), rows),
            pl.BlockSpec((tm, D_MODEL), rows),
            pl.BlockSpec((1, D_MODEL, D_FF), lambda i, e: (e, 0, 0)),
            pl.BlockSpec((1, D_MODEL, D_FF), lambda i, e: (e, 0, 0)),
            pl.BlockSpec((1, D_FF, D_MODEL), lambda i, e: (e, 0, 0)),
            pl.BlockSpec((1, D_MODEL), lambda i, e: (0, 0)),
        ],
        out_specs=out_specs,
        out_shape=out_shape,
        compiler_params=_params(2),
        name="moe",
    )(xn, gate, x1, w1, w3, w2, nfin)


def _rope_tables(pos):
    half = D_HEAD // 2
    inv = jnp.asarray((ROPE_BASE ** (-np.arange(half, dtype=np.float64) / half)).astype(np.float32))
    ang = pos.astype(F32)[:, None] * inv[None, :]
    cos, sin = jnp.cos(ang), jnp.sin(ang)
    return jnp.concatenate([cos, cos], axis=-1), jnp.concatenate([-sin, sin], axis=-1)


def _retention_tables(t_eff):
    idx = jnp.arange(CHUNK, dtype=F32)
    log_gamma = jnp.log(1.0 - jnp.exp2(-5.0 - jnp.arange(N_HEADS, dtype=F32)))
    incl = idx[:, None] >= idx[None, :]
    rel = jnp.where(incl, idx[:, None] - idx[None, :], 0.0)
    dmat = jnp.where(incl, jnp.exp(rel[None] * log_gamma[:, None, None]), 0.0)
    cross = jnp.exp((idx + 1.0)[:, None] * log_gamma[None, :])
    kdec = jnp.exp((t_eff - 1.0 - idx)[:, None] * log_gamma[None, :])
    tot = jnp.broadcast_to(jnp.exp(t_eff * log_gamma)[None, :], (CHUNK, N_HEADS))
    rtab = jnp.concatenate([cross, kdec, tot, jnp.zeros((CHUNK, LANES - 3 * N_HEADS), F32)], axis=1)
    return dmat, rtab


def _lanes_row(pairs):
    row = jnp.zeros((LANES,), F32)
    for start, vals in pairs:
        row = row.at[start:start + vals.shape[0]].set(vals.astype(F32))
    return row


def _pick(n, candidates):
    for c in candidates:
        if n % c == 0:
            return c
    raise ValueError(f"no tile in {candidates} divides {n}")


def kernel(x_prompt, x_sample, cache_k, cache_v, cache_logf, page_table, state_delta, state_conv, state_ret,
           meta_tokens, norm_mix, norm_ffn, norm_final, w_in, conv_w, a_log, dt_bias, a_norm, b_fbias, c_norm,
           w_branch, w_out, w_router_group, b_router_group, w_router_expert, b_router_expert, w1, w3, w2):
    nb, seq, _ = x_prompt.shape
    ns, ts, _ = x_sample.shape
    depth, n_pool = cache_k.shape[:2]
    n_pages = page_table.shape[1]
    assert seq % CHUNK == 0 and ts <= SAMPLE_ROWS and cache_k.shape[2] == CHUNK
    lp = CHUNK + seq
    n_chunks = lp // CHUNK
    rp = nb * lp
    rs = ns * SAMPLE_ROWS
    tm_p = _pick(rp, (512, 256, 128))
    tm_s = _pick(rs, (256, 128, 64, 32, 16, 8))

    xp = jnp.concatenate([jnp.zeros((nb, PAD_FRONT, D_MODEL), F32),
                          jnp.broadcast_to(meta_tokens[None], (nb, N_META, D_MODEL)), x_prompt],
                         axis=1).reshape(rp, D_MODEL)
    xs = jnp.concatenate([x_sample, jnp.zeros((ns, SAMPLE_ROWS - ts, D_MODEL), F32)], axis=1).reshape(rs, D_MODEL)

    cos_p, sin_p = _rope_tables(jnp.arange(lp) - PAD_FRONT)
    cos_s, sin_s = _rope_tables(n_pages * CHUNK + jnp.arange(CHUNK))
    dmat_p, rtab_p = _retention_tables(float(CHUNK))
    dmat_s, rtab_s = _retention_tables(float(ts))

    assert (ts * N_HEADS) % 8 == 0
    n_cache = depth * n_pool
    cache_k2 = cache_k.reshape(n_cache * CHUNK * N_HEADS, D_HEAD)
    cache_v2 = cache_v.reshape(n_cache * CHUNK * N_HEADS, D_HEAD)
    c_page, tot_page = _page_cumsum(cache_logf.reshape(n_cache, CHUNK * N_HEADS))
    c_page = c_page.reshape(n_cache, 1, CHUNK * N_HEADS)
    tot_page = tot_page.reshape(n_cache, 1, CHUNK * N_HEADS)
    tb = _pick(lp, (384, 256, 128))
    par_p = _pick(nb, (PAR, 1))
    par_s = _pick(ns, (PAR, 1))
    ppg = _pick(n_pages, (8, 4, 2, 1))

    o_beta = A_QKV + BRANCH_W
    o_bq = o_beta + 8
    o_f = o_bq + 3 * BRANCH_W
    o_gates = o_f + 4 + 4 * BRANCH_W

    def sample_heads(a):
        return a.reshape(ns, SAMPLE_ROWS, N_HEADS, D_HEAD)[:, :ts].reshape(ns, ts * N_HEADS, D_HEAD)

    def prompt_rows(a):
        return a.reshape(nb, lp, a.shape[-1])[:, PAD_FRONT:]

    def sample_rows(a):
        return a.reshape(ns, SAMPLE_ROWS, a.shape[-1])[:, :ts]

    def col(a, c, w=BRANCH_W):
        return a[:, c * BRANCH_W:c * BRANCH_W + w]

    outs_p = [[] for _ in range(6)]
    outs_s = [[] for _ in range(6)]
    yp = ys = None
    for l in range(depth):
        final = l == depth - 1
        hp = not final
        wdt = F32 if hp else BF16
        wl = w_in[l]
        w_main = jnp.concatenate([wl[:, o_gates:], wl[:, :o_beta], wl[:, o_bq:o_f], wl[:, o_f + 4:o_gates]],
                                 axis=1).astype(wdt)
        w_small = jnp.concatenate([wl[:, o_beta:o_bq], wl[:, o_f:o_f + 4],
                                   jnp.zeros((D_MODEL, LANES - 12), F32)], axis=1).astype(wdt)
        g_mix = norm_mix[l][None]
        p_p, ps_p = _proj(xp, g_mix, w_main, w_small, tm_p, 512, hp)
        p_s, ps_s = _proj(xs, g_mix, w_main, w_small, tm_s, 512, hp)

        avec = jnp.zeros((8, LANES), F32)
        avec = avec.at[0].set(_lanes_row([(LANE_ALPHA, a_log[l])]))
        avec = avec.at[1].set(_lanes_row([(LANE_ALPHA, dt_bias[l])]))
        avec = avec.at[2].set(_lanes_row([(LANE_F, b_fbias[l])]))
        shared = (conv_w[l], avec, a_norm[l][None], c_norm[l][None])
        oa_p, oc_p, lf_p, cc_p, sd_p, sr_p = _seq(
            p_p, ps_p, cos_p, sin_p, *shared, dmat_p, rtab_p, n_seq=nb, n_chunks=n_chunks, tc_in=CHUNK,
            valid_hi=CHUNK, front_pad=PAD_FRONT, precise=hp, n_par=par_p)
        cb0 = jnp.pad(state_conv[l], ((0, 0), (8 - (CONV_W - 1), 0), (0, 0)))
        oa_s, oc_s, lf_s, cc_s, sd_s, sr_s = _seq(
            p_s, ps_s, cos_s, sin_s, *shared, dmat_s, rtab_s, n_seq=ns, n_chunks=1, tc_in=SAMPLE_ROWS,
            valid_hi=ts, front_pad=0, precise=hp, n_par=par_s, init=(state_delta[l], state_ret[l], cb0))

        ct = jnp.pad(jnp.swapaxes(cc_p[:, LANE_F:LANE_F + N_HEADS].reshape(nb, lp, N_HEADS), 1, 2),
                     ((0, 0), (0, 8 - N_HEADS), (0, 0)))
        ob_p = _fox_prompt(p_p, ct, nb, lp, tb, hp)
        cn = cc_s[:, LANE_F:LANE_F + N_HEADS].reshape(ns, SAMPLE_ROWS, N_HEADS)[:, :ts].reshape(ns, 1, ts * N_HEADS)
        cn = jnp.pad(cn, ((0, 0), (0, 0), (0, CHUNK - ts * N_HEADS)))
        o_s = _fox_sample(page_table, sample_heads(col(p_s, COL_BQ)), sample_heads(col(p_s, COL_BK)),
                          sample_heads(col(p_s, COL_BV)), cn, cache_k2, cache_v2, c_page, tot_page,
                          layer=l, n_pool=n_pool, ppg=ppg, precise=hp)
        ob_s = jnp.pad(o_s.reshape(ns, ts, BRANCH_W), ((0, 0), (0, SAMPLE_ROWS - ts), (0, 0))).reshape(rs, BRANCH_W)

        wr = jnp.concatenate([w_router_group[l], w_router_expert[l],
                              jnp.zeros((D_MODEL, LANES - N_GROUPS - N_EXPERTS), F32)], axis=1)
        br = _lanes_row([(LANE_GROUP, b_router_group[l]), (LANE_EXPERT, b_router_expert[l].reshape(-1))])[None]
        nf = norm_ffn[l][None]
        wb, wo = w_branch[l].astype(wdt), w_out[l].astype(wdt)
        x1_p, xn_p, gate_p = _merge(oa_p, ob_p, oc_p, p_p, xp, wb, wo, nf, wr, br, min(tm_p, 256), hp)
        x1_s, xn_s, gate_s = _merge(oa_s, ob_s, oc_s, p_s, xs, wb, wo, nf, wr, br, tm_s, hp)

        ew1 = w1[l].reshape(N_EXPERTS, D_MODEL, D_FF).astype(BF16)
        ew3 = w3[l].reshape(N_EXPERTS, D_MODEL, D_FF).astype(BF16)
        ew2 = w2[l].reshape(N_EXPERTS, D_FF, D_MODEL).astype(BF16)
        res_p = _moe(xn_p, gate_p, x1_p, ew1, ew3, ew2, norm_final[None], tm_p, final)
        res_s = _moe(xn_s, gate_s, x1_s, ew1, ew3, ew2, norm_final[None], tm_s, final)
        xp, xs = res_p[0], res_s[0]
        if final:
            yp, ys = res_p[1], res_s[1]

        for lst, rows_of, p, lf_a, sd, sr in ((outs_p, prompt_rows, p_p, lf_p, sd_p, sr_p),
                                              (outs_s, sample_rows, p_s, lf_s, sd_s, sr_s)):
            k_new = rows_of(col(p, COL_BK))
            v_new = rows_of(col(p, COL_BV))
            lst[0].append(k_new.reshape(k_new.shape[:2] + (N_HEADS, D_HEAD)))
            lst[1].append(v_new.reshape(v_new.shape[:2] + (N_HEADS, D_HEAD)))
            lst[2].append(rows_of(lf_a[:, LANE_F:LANE_F + N_HEADS]))
            lst[3].append(sd)
            lst[4].append(rows_of(col(p, COL_AQKV, A_QKV))[:, -(CONV_W - 1):])
            lst[5].append(sr)

    y_prompt = yp.reshape(nb, lp, D_MODEL)[:, CHUNK:]
    y_sample = ys.reshape(ns, SAMPLE_ROWS, D_MODEL)[:, :ts]
    return (y_prompt, y_sample, *(jnp.stack(a) for a in outs_p), *(jnp.stack(a) for a in outs_s))
```

```python
import functools

import jax
import jax.numpy as jnp
import numpy as np
from jax import lax
from jax.experimental import pallas as pl
from jax.experimental.pallas import tpu as pltpu

F32 = jnp.float32
BF16 = jnp.bfloat16

D_MODEL = 1024
N_META = 16
CHUNK = 128
PAD_FRONT = CHUNK - N_META
N_HEADS = 4
D_HEAD = 128
BRANCH_W = N_HEADS * D_HEAD
A_QKV = 3 * BRANCH_W
CONV_W = 4
N_BRANCH = 3
N_GROUPS = 4
EXP_PER_GROUP = 4
N_EXPERTS = N_GROUPS * EXP_PER_GROUP
D_FF = 512
ROPE_BASE = 10000.0
EPS = 1e-6
NEG = -1e30
SAMPLE_ROWS = 8
LANES = 128

COL_GATES = 0
COL_AQKV = 6
COL_AZ = 9
COL_BQ, COL_BK, COL_BV = 10, 11, 12
COL_CQ, COL_CK, COL_CV, COL_CG = 13, 14, 15, 16
N_MAIN = 17 * BRANCH_W
LANE_BETA, LANE_ALPHA, LANE_F = 0, 4, 8
LANE_GROUP, LANE_EXPERT = 0, 4

VMEM_LIMIT = 56 * 1024 * 1024
PAR_PROMPT, PAR_SAMPLE = 2, 4

NN = (((1,), (0,)), ((), ()))
NT = (((1,), (1,)), ((), ()))
TN = (((0,), (0,)), ((), ()))


def _split2(x):
    hi = x.astype(BF16)
    return hi, (x - hi.astype(F32)).astype(BF16)


def _dot(a, b, dims=NN, precise=False):
    def dg(x, y):
        return lax.dot_general(x, y, dims, preferred_element_type=F32)

    if not precise:
        a = a[0] if isinstance(a, tuple) else a.astype(BF16)
        b = b[0] if isinstance(b, tuple) else b.astype(BF16)
        return dg(a, b)
    a_hi, a_lo = a if isinstance(a, tuple) else _split2(a)
    b_hi, b_lo = b if isinstance(b, tuple) else _split2(b)
    return dg(a_hi, b_hi) + dg(a_hi, b_lo) + dg(a_lo, b_hi)


def _split3(x):
    x1 = x.astype(BF16)
    r1 = x - x1.astype(F32)
    x2 = r1.astype(BF16)
    x3 = (r1 - x2.astype(F32)).astype(BF16)
    return x1, x2, x3


def _sigmoid(x):
    return 1.0 / (1.0 + jnp.exp(-x))


def _silu(x):
    return x * _sigmoid(x)


def _softplus(x):
    return jnp.maximum(x, 0.0) + jnp.log1p(jnp.exp(-jnp.abs(x)))


def _rms(x, g):
    return x * lax.rsqrt(jnp.mean(x * x, axis=-1, keepdims=True) + EPS) * g


def _params(n_axes):
    return pltpu.CompilerParams(dimension_semantics=("arbitrary",) * n_axes, vmem_limit_bytes=VMEM_LIMIT)


SEGMENTS = ((0, A_QKV + 8 * BRANCH_W + 12, N_BRANCH * D_MODEL),
            (N_BRANCH * D_MODEL, 0, A_QKV + BRANCH_W),
            (N_BRANCH * D_MODEL + 4 * BRANCH_W, A_QKV + BRANCH_W + 8, 3 * BRANCH_W),
            (N_BRANCH * D_MODEL + 7 * BRANCH_W, A_QKV + 4 * BRANCH_W + 12, 4 * BRANCH_W))
OLD_SMALL_A = A_QKV + BRANCH_W
OLD_SMALL_F = A_QKV + 4 * BRANCH_W + 8


def _reorder_kernel(blk_ref, sh_ref, a_ref, b_ref, o_ref, *, shifts):
    j = pl.program_id(0)
    for shift in shifts:
        @pl.when(sh_ref[j] == shift)
        def _(shift=shift):
            a = a_ref[...]
            if shift:
                a = jnp.concatenate([a[:, shift:], b_ref[:, :shift]], axis=1)
            o_ref[...] = a.astype(o_ref.dtype)


def _small_kernel(a_ref, f_ref, o_ref):
    lane = lax.broadcasted_iota(jnp.int32, a_ref.shape, 1)
    o_ref[...] = jnp.where(lane < LANE_F, a_ref[...], jnp.where(lane < LANE_F + N_HEADS, f_ref[...], 0.0)
                           ).astype(o_ref.dtype)


def _reorder_w_in(w_in, layer, dtype):
    n_old = w_in.shape[2]
    last = (n_old - 1) // LANES
    blks, shifts = [], []
    for new0, old0, width in SEGMENTS:
        assert new0 % LANES == 0 and width % LANES == 0
        for t in range(width // LANES):
            blks.append(old0 // LANES + t)
            shifts.append(old0 % LANES)
    grid_spec = pltpu.PrefetchScalarGridSpec(
        num_scalar_prefetch=2,
        grid=(len(blks),),
        in_specs=[pl.BlockSpec((None, D_MODEL, LANES), lambda j, blk, sh: (layer, 0, blk[j])),
                  pl.BlockSpec((None, D_MODEL, LANES), lambda j, blk, sh: (layer, 0, jnp.minimum(blk[j] + 1, last)))],
        out_specs=pl.BlockSpec((D_MODEL, LANES), lambda j, blk, sh: (0, j)),
    )
    w_main = pl.pallas_call(
        functools.partial(_reorder_kernel, shifts=tuple(sorted(set(shifts)))),
        grid_spec=grid_spec,
        out_shape=jax.ShapeDtypeStruct((D_MODEL, N_MAIN), dtype),
        compiler_params=_params(1),
        name="reorder_w_in",
    )(jnp.asarray(blks, jnp.int32), jnp.asarray(shifts, jnp.int32), w_in, w_in)
    assert OLD_SMALL_A % LANES == LANE_BETA and OLD_SMALL_F % LANES == LANE_F
    w_small = pl.pallas_call(
        _small_kernel,
        grid=(1,),
        in_specs=[pl.BlockSpec((None, D_MODEL, LANES), lambda i: (layer, 0, OLD_SMALL_A // LANES)),
                  pl.BlockSpec((None, D_MODEL, LANES), lambda i: (layer, 0, OLD_SMALL_F // LANES))],
        out_specs=pl.BlockSpec((D_MODEL, LANES), lambda i: (0, 0)),
        out_shape=jax.ShapeDtypeStruct((D_MODEL, LANES), dtype),
        compiler_params=_params(1),
        name="reorder_w_small",
    )(w_in, w_in)
    return w_main, w_small


def _proj_kernel(x_ref, g_ref, w_ref, ws_ref, p_ref, ps_ref, xn_ref, *, tn, precise):
    j = pl.program_id(1)

    @pl.when(j == 0)
    def _():
        xn = _split2(_rms(x_ref[...], g_ref[...]))
        xn_ref[0] = xn[0]
        if precise:
            xn_ref[1] = xn[1]
        ps_ref[...] = _dot(xn, ws_ref[...], precise=precise)

    if precise:
        p_ref[...] = _dot((xn_ref[0], xn_ref[1]), w_ref[...], precise=True)
    else:
        p_ref[...] = _dot(xn_ref[0], w_ref[:, pl.ds(pl.multiple_of(j * tn, tn), tn)])


def _proj(x, g, w_main, w_small, tm, tn, precise):
    r = x.shape[0]
    w_spec = (pl.BlockSpec((D_MODEL, tn), lambda i, j: (0, j)) if precise else
              pl.BlockSpec((D_MODEL, N_MAIN), lambda i, j: (0, 0)))
    return pl.pallas_call(
        functools.partial(_proj_kernel, tn=tn, precise=precise),
        grid=(r // tm, N_MAIN // tn),
        in_specs=[pl.BlockSpec((tm, D_MODEL), lambda i, j: (i, 0)),
                  pl.BlockSpec((1, D_MODEL), lambda i, j: (0, 0)),
                  w_spec,
                  pl.BlockSpec((D_MODEL, LANES), lambda i, j: (0, 0))],
        out_specs=[pl.BlockSpec((tm, tn), lambda i, j: (i, j)),
                   pl.BlockSpec((tm, LANES), lambda i, j: (i, 0))],
        out_shape=[jax.ShapeDtypeStruct((r, N_MAIN), F32), jax.ShapeDtypeStruct((r, LANES), F32)],
        scratch_shapes=[pltpu.VMEM((2 if precise else 1, tm, D_MODEL), BF16)],
        compiler_params=_params(2),
        name="proj_split" if precise else "proj",
    )(x, g, w_main, w_small)


def _delta_heads(qs, ks, vs, betas, gcols, grows, states, incl, strict, precise):
    mm = functools.partial(_dot, precise=precise)
    n = range(len(qs))
    egs = [jnp.exp(gcols[i]) for i in n]
    decays = [jnp.where(incl, jnp.exp(jnp.where(incl, gcols[i] - grows[i], 0.0)), 0.0) for i in n]
    kbs = [ks[i] * betas[i] for i in n]
    ms = [jnp.where(strict, mm(kbs[i], ks[i], NT) * decays[i], 0.0) for i in n]
    rs = [-m for m in ms]
    pows = [_dot(r, r) for r in rs]
    for it in range(6):
        rps = [_dot(rs[i], pows[i]) for i in n]
        if it < 5:
            nxt = [_dot(pows[i], pows[i]) for i in n]
        rs = [rs[i] + pows[i] + rps[i] for i in n]
        if it < 5:
            pows = nxt
    for _ in range(2 if precise else 1):
        es = [-(ms[i] + rs[i] + _dot(ms[i], rs[i], precise=True)) for i in n]
        res = [_dot(rs[i], es[i]) for i in n]
        rs = [rs[i] + es[i] + res[i] for i in n]
    vbs = [vs[i] * betas[i] for i in n]
    kbes = [kbs[i] * egs[i] for i in n]
    us = [vbs[i] + mm(rs[i], vbs[i]) for i in n]
    ws = [kbes[i] + mm(rs[i], kbes[i]) for i in n]
    attns = [mm(qs[i], ks[i], NT) * decays[i] for i in n]
    v_news = [us[i] - mm(ws[i], states[i]) for i in n]
    outs = [mm(qs[i] * egs[i], states[i]) + mm(attns[i], v_news[i]) for i in n]
    g_lasts = [g[CHUNK - 1:CHUNK, :] for g in gcols]
    new_states = [states[i] * jnp.exp(g_lasts[i]) + mm(ks[i] * jnp.exp(g_lasts[i] - gcols[i]), v_news[i], TN)
                  for i in n]
    return outs, new_states


def _seq_kernel(*refs, n_par, tc_in, valid_hi, front_pad, has_init, precise):
    if has_init:
        (aqkv_ref, az_ref, cq_ref, ck_ref, cv_ref, cg_ref, sm_ref, cos_ref, sin_ref,
         convw_ref, avec_ref, anorm_ref, cnorm_ref, dmat_ref, rtab_ref,
         sd0_ref, sr0_ref, cb0_ref,
         oa_ref, oc_ref, lf_ref, cc_ref, sd_ref, sr_ref, xbuf, cbuf, carry) = refs
    else:
        (aqkv_ref, az_ref, cq_ref, ck_ref, cv_ref, cg_ref, sm_ref, cos_ref, sin_ref,
         convw_ref, avec_ref, anorm_ref, cnorm_ref, dmat_ref, rtab_ref,
         oa_ref, oc_ref, lf_ref, cc_ref, sd_ref, sr_ref, xbuf, cbuf, carry) = refs
    mm = functools.partial(_dot, precise=precise)
    c = pl.program_id(1)

    def ext(ref, s, cols=None):
        v = ref[s] if cols is None else ref[s, :, cols]
        if tc_in == CHUNK:
            return v
        return jnp.concatenate([v, jnp.zeros((CHUNK - tc_in, v.shape[1]), v.dtype)], axis=0)

    @pl.when(c == 0)
    def _():
        carry[...] = jnp.zeros_like(carry)
        if has_init:
            xbuf[:, 0:8, :] = cb0_ref[...]
            sd_ref[...] = sd0_ref[...]
            sr_ref[...] = sr0_ref[...]
        else:
            xbuf[:, 0:8, :] = jnp.zeros((n_par, 8, A_QKV), F32)
            sd_ref[...] = jnp.zeros_like(sd_ref)
            sr_ref[...] = jnp.zeros_like(sr_ref)

    rows = lax.broadcasted_iota(jnp.int32, (CHUNK, LANES), 0)
    lane = lax.broadcasted_iota(jnp.int32, (CHUNK, LANES), 1)
    lo = jnp.where(c == 0, front_pad, 0)
    valid = (rows >= lo) & (rows < valid_hi)
    incl = rows >= lane
    strict = rows > lane
    is_g = (lane >= LANE_ALPHA) & (lane < LANE_ALPHA + N_HEADS)
    is_f = (lane >= LANE_F) & (lane < LANE_F + N_HEADS)
    tril = jnp.where(incl, 1.0, 0.0).astype(BF16)
    av = avec_ref[...]
    cos2 = cos_ref[...]
    sin2 = sin_ref[...]
    rtab = rtab_ref[...]

    heads = range(N_HEADS)
    hsl = [slice(h * D_HEAD, (h + 1) * D_HEAD) for h in heads]
    chains = [(s, h) for s in range(n_par) for h in heads]
    qs, ks, vs, betas, gcols, grows = [], [], [], [], [], []
    for s in range(n_par):
        for j in range(A_QKV // LANES):
            cols = slice(j * LANES, (j + 1) * LANES)
            xbuf[s, 8:8 + CHUNK, cols] = ext(aqkv_ref, s, cols)
            acc = xbuf[s, 5:5 + CHUNK, cols] * convw_ref[0:1, cols]
            for i in range(1, CONV_W):
                acc = acc + xbuf[s, 5 + i:5 + i + CHUNK, cols] * convw_ref[i:i + 1, cols]
            cbuf[s, :, cols] = jnp.where(valid, _silu(acc), 0.0)
            xbuf[s, 0:8, cols] = xbuf[s, CHUNK:CHUNK + 8, cols]

        sm = ext(sm_ref, s)
        g_all = -jnp.exp(av[0:1, :]) * _softplus(sm + av[1:2, :])
        lf_all = -_softplus(-(sm + av[2:3, :]))
        z = jnp.where(valid & is_g, g_all, jnp.where(valid & is_f, lf_all, 0.0))
        cum = sum(jnp.dot(tril, part, preferred_element_type=F32) for part in _split3(z))
        cum_t = cum.T
        cc = cum + carry[s]
        carry[s] = cc[CHUNK - 1:CHUNK, :]
        lf_ref[s] = z[:tc_in]
        cc_ref[s] = cc[:tc_in]

        beta_all = _sigmoid(sm)
        for h in heads:
            q = cbuf[s, :, hsl[h]]
            k = cbuf[s, :, BRANCH_W + h * D_HEAD:BRANCH_W + (h + 1) * D_HEAD]
            qs.append(q * lax.rsqrt(jnp.sum(q * q, axis=-1, keepdims=True) + EPS) * (D_HEAD ** -0.5))
            ks.append(k * lax.rsqrt(jnp.sum(k * k, axis=-1, keepdims=True) + EPS))
            vs.append(cbuf[s, :, 2 * BRANCH_W + h * D_HEAD:2 * BRANCH_W + (h + 1) * D_HEAD])
            betas.append(beta_all[:, LANE_BETA + h:LANE_BETA + h + 1])
            gcols.append(cum[:, LANE_ALPHA + h:LANE_ALPHA + h + 1])
            grows.append(cum_t[LANE_ALPHA + h:LANE_ALPHA + h + 1, :])

    outs, new_states = _delta_heads(qs, ks, vs, betas, gcols, grows, [sd_ref[s, h] for s, h in chains],
                                    incl, strict, precise)
    for i, (s, h) in enumerate(chains):
        sd_ref[s, h] = new_states[i]
        oa = _rms(outs[i], anorm_ref[...]) * _silu(ext(az_ref, s, hsl[h]))
        oa_ref[s, :, hsl[h]] = oa[:tc_in]

    qcs = [ext(cq_ref, s, hsl[h]) for s, h in chains]
    kcs = [ext(ck_ref, s, hsl[h]) for s, h in chains]
    vcs = [ext(cv_ref, s, hsl[h]) for s, h in chains]
    qcs = [q * cos2 + pltpu.roll(q, D_HEAD // 2, 1) * sin2 for q in qcs]
    kcs = [(k * cos2 + pltpu.roll(k, D_HEAD // 2, 1) * sin2) * (D_HEAD ** -0.5) for k in kcs]
    srs = [sr_ref[s, h] for s, h in chains]
    scores = [mm(qcs[i], kcs[i], NT) * dmat_ref[h] for i, (s, h) in enumerate(chains)]
    crosses = [mm(qcs[i] * rtab[:, h:h + 1], srs[i]) for i, (s, h) in enumerate(chains)]
    kvs = [mm(kcs[i] * rtab[:, 4 + h:5 + h], vcs[i], TN) for i, (s, h) in enumerate(chains)]
    inners = [mm(scores[i], vcs[i]) for i in range(len(chains))]
    for i, (s, h) in enumerate(chains):
        sr_ref[s, h] = srs[i] * rtab[0:1, 8 + h:9 + h] + kvs[i]
        oc = _rms(inners[i] + crosses[i], cnorm_ref[...]) * _silu(ext(cg_ref, s, hsl[h]))
        oc_ref[s, :, hsl[h]] = oc[:tc_in]


def _seq(p, ps, cos2, sin2, convw, avec, anorm, cnorm, dmat, rtab, *, n_seq, n_chunks, tc_in,
         valid_hi, front_pad, precise, n_par, init=None):
    r = p.shape[0]
    lp = n_chunks * tc_in
    p3 = p.reshape(n_seq, lp, N_MAIN)
    ps3 = ps.reshape(n_seq, lp, LANES)

    def rows(width, col):
        return pl.BlockSpec((n_par, tc_in, width), lambda n, c: (n, c, col))

    const2 = lambda n, c: (0, 0)
    state_spec = pl.BlockSpec((n_par, N_HEADS, D_HEAD, D_HEAD), lambda n, c: (n, 0, 0, 0))
    in_specs = [
        rows(A_QKV, COL_AQKV // 3),
        rows(BRANCH_W, COL_AZ),
        rows(BRANCH_W, COL_CQ),
        rows(BRANCH_W, COL_CK),
        rows(BRANCH_W, COL_CV),
        rows(BRANCH_W, COL_CG),
        rows(LANES, 0),
        pl.BlockSpec((CHUNK, D_HEAD), lambda n, c: (c, 0)),
        pl.BlockSpec((CHUNK, D_HEAD), lambda n, c: (c, 0)),
        pl.BlockSpec((CONV_W, A_QKV), const2),
        pl.BlockSpec((8, LANES), const2),
        pl.BlockSpec((1, D_HEAD), const2),
        pl.BlockSpec((1, D_HEAD), const2),
        pl.BlockSpec((N_HEADS, CHUNK, CHUNK), lambda n, c: (0, 0, 0)),
        pl.BlockSpec((CHUNK, LANES), const2),
    ]
    args = [p3, p3, p3, p3, p3, p3, ps3, cos2, sin2, convw, avec, anorm, cnorm, dmat, rtab]
    if init is not None:
        in_specs += [state_spec, state_spec, pl.BlockSpec((n_par, 8, A_QKV), lambda n, c: (n, 0, 0))]
        args += list(init)
    out_specs = [rows(BRANCH_W, 0), rows(BRANCH_W, 0), rows(LANES, 0), rows(LANES, 0), state_spec, state_spec]
    out_shape = [
        jax.ShapeDtypeStruct((n_seq, lp, BRANCH_W), F32),
        jax.ShapeDtypeStruct((n_seq, lp, BRANCH_W), F32),
        jax.ShapeDtypeStruct((n_seq, lp, LANES), F32),
        jax.ShapeDtypeStruct((n_seq, lp, LANES), F32),
        jax.ShapeDtypeStruct((n_seq, N_HEADS, D_HEAD, D_HEAD), F32),
        jax.ShapeDtypeStruct((n_seq, N_HEADS, D_HEAD, D_HEAD), F32),
    ]
    kern = functools.partial(_seq_kernel, n_par=n_par, tc_in=tc_in, valid_hi=valid_hi, front_pad=front_pad,
                             has_init=init is not None, precise=precise)
    oa, oc, lf, cc, sd, sr = pl.pallas_call(
        kern,
        grid=(n_seq // n_par, n_chunks),
        in_specs=in_specs,
        out_specs=out_specs,
        out_shape=out_shape,
        scratch_shapes=[pltpu.VMEM((n_par, CHUNK + 8, A_QKV), F32), pltpu.VMEM((n_par, CHUNK, A_QKV), F32),
                        pltpu.VMEM((n_par, 1, LANES), F32)],
        compiler_params=_params(2),
        name=("seq_prompt" if init is None else "seq_sample") + ("_split" if precise else ""),
    )(*args)
    return (oa.reshape(r, BRANCH_W), oc.reshape(r, BRANCH_W), lf.reshape(r, LANES), cc.reshape(r, LANES), sd, sr)


def _fox_prompt_kernel(q_ref, k_ref, v_ref, ct_ref, o_ref, m_ref, l_ref, acc_ref, *, tb, precise):
    mm = functools.partial(_dot, precise=precise)
    i = pl.program_id(1)
    j = pl.program_id(2)

    @pl.when(j == 0)
    def _():
        m_ref[...] = jnp.full_like(m_ref, NEG)
        l_ref[...] = jnp.zeros_like(l_ref)
        acc_ref[...] = jnp.zeros_like(acc_ref)

    def block(masked):
        if masked:
            qpos = i * tb + lax.broadcasted_iota(jnp.int32, (tb, tb), 0)
            kpos = j * tb + lax.broadcasted_iota(jnp.int32, (tb, tb), 1)
            ok = (kpos <= qpos) & (kpos >= PAD_FRONT)
        for h in range(N_HEADS):
            hs = slice(h * D_HEAD, (h + 1) * D_HEAD)
            s = mm(q_ref[:, hs] * (D_HEAD ** -0.5), k_ref[:, hs], NT) - ct_ref[0, h:h + 1, :]
            if masked:
                s = jnp.where(ok, s, NEG)
            m_old = m_ref[h][:, 0:1]
            m_new = jnp.maximum(m_old, jnp.max(s, axis=-1, keepdims=True))
            alpha = jnp.exp(m_old - m_new)
            p = jnp.exp(s - m_new)
            l_ref[h] = alpha * l_ref[h] + jnp.sum(p, axis=-1, keepdims=True)
            acc_ref[h] = alpha * acc_ref[h] + mm(p, v_ref[:, hs])
            m_ref[h] = jnp.broadcast_to(m_new, (tb, LANES))

    needs_mask = (j == i) | (j == 0)
    pl.when(needs_mask)(functools.partial(block, True))
    pl.when((j < i) & (j > 0))(functools.partial(block, False))

    @pl.when(j == i)
    def _():
        qvalid = (i * tb + lax.broadcasted_iota(jnp.int32, (tb, D_HEAD), 0)) >= PAD_FRONT
        for h in range(N_HEADS):
            o = acc_ref[h] / l_ref[h][:, 0:1]
            o_ref[:, h * D_HEAD:(h + 1) * D_HEAD] = jnp.where(qvalid, o, 0.0)


def _fox_prompt(p, ct, n_seq, lp, tb, precise):
    r = p.shape[0]
    nb = lp // tb
    return pl.pallas_call(
        functools.partial(_fox_prompt_kernel, tb=tb, precise=precise),
        grid=(n_seq, nb, nb),
        in_specs=[
            pl.BlockSpec((tb, BRANCH_W), lambda n, i, j: (n * nb + i, COL_BQ)),
            pl.BlockSpec((tb, BRANCH_W), lambda n, i, j: (n * nb + jnp.minimum(i, j), COL_BK)),
            pl.BlockSpec((tb, BRANCH_W), lambda n, i, j: (n * nb + jnp.minimum(i, j), COL_BV)),
            pl.BlockSpec((1, 8, tb), lambda n, i, j: (n, 0, jnp.minimum(i, j))),
        ],
        out_specs=pl.BlockSpec((tb, BRANCH_W), lambda n, i, j: (n * nb + i, 0)),
        out_shape=jax.ShapeDtypeStruct((r, BRANCH_W), F32),
        scratch_shapes=[pltpu.VMEM((N_HEADS, tb, LANES), F32), pltpu.VMEM((N_HEADS, tb, LANES), F32),
                        pltpu.VMEM((N_HEADS, tb, D_HEAD), F32)],
        compiler_params=_params(3),
        name="fox_prompt",
    )(p, p, p, ct)


def _page_cumsum_kernel(lf_ref, c_ref, tot_ref):
    n = CHUNK * N_HEADS
    ri = lax.broadcasted_iota(jnp.int32, (n, n), 0)
    ci = lax.broadcasted_iota(jnp.int32, (n, n), 1)
    same_head = (ri % N_HEADS) == (ci % N_HEADS)
    upper = jnp.where(same_head & (ri <= ci), 1.0, 0.0).astype(BF16)
    total = jnp.where(same_head, 1.0, 0.0).astype(BF16)
    parts = _split3(lf_ref[...])
    c_ref[...] = sum(jnp.dot(x, upper, preferred_element_type=F32) for x in parts)
    tot_ref[...] = sum(jnp.dot(x, total, preferred_element_type=F32) for x in parts)


def _page_cumsum(lf):
    n_pages, n = lf.shape
    tp = _pick(n_pages, (256, 128, 64, 32, 16, 8, n_pages))
    spec = pl.BlockSpec((tp, n), lambda i: (i, 0))
    return pl.pallas_call(
        _page_cumsum_kernel,
        grid=(n_pages // tp,),
        in_specs=[spec],
        out_specs=[spec, spec],
        out_shape=[jax.ShapeDtypeStruct((n_pages, n), F32)] * 2,
        compiler_params=_params(1),
        name="page_cumsum",
    )(lf)


def _fox_sample_kernel(*refs, ppg, n_q, precise):
    q_ref, kn_ref, vn_ref, cn_ref = refs[1:5]
    k_refs = refs[5:5 + ppg]
    v_refs = refs[5 + ppg:5 + 2 * ppg]
    c_refs = refs[5 + 2 * ppg:5 + 3 * ppg]
    tot_refs = refs[5 + 3 * ppg:5 + 4 * ppg]
    o_ref, m_ref, l_ref, acc_ref, carry = refs[5 + 4 * ppg:]
    mm = functools.partial(_dot, precise=precise)
    g = pl.program_id(1)
    n_keys = CHUNK * N_HEADS

    @pl.when(g == 0)
    def _():
        m_ref[...] = jnp.full_like(m_ref, NEG)
        l_ref[...] = jnp.zeros_like(l_ref)
        acc_ref[...] = jnp.zeros_like(acc_ref)
        carry[...] = jnp.zeros_like(carry)

    row = lax.broadcasted_iota(jnp.int32, (n_q, n_keys), 0)
    lane = lax.broadcasted_iota(jnp.int32, (n_q, n_keys), 1)
    same_head = (row % N_HEADS) == (lane % N_HEADS)
    scale = D_HEAD ** -0.5
    q = _split2(q_ref[0])

    def update(scores, values):
        top = scores[0]
        for s in scores[1:]:
            top = jnp.maximum(top, s)
        m_old = m_ref[:, 0:1]
        m_new = jnp.maximum(m_old, jnp.max(top, axis=-1, keepdims=True))
        alpha = jnp.exp(m_old - m_new)
        probs = [jnp.exp(s - m_new) for s in scores]
        mass = probs[0]
        for p in probs[1:]:
            mass = mass + p
        pv = mm(probs[0], values[0])
        for p, v in zip(probs[1:], values[1:]):
            pv = pv + mm(p, v)
        l_ref[...] = alpha * l_ref[...] + jnp.sum(mass, axis=-1, keepdims=True)
        acc_ref[...] = alpha * acc_ref[...] + pv
        m_ref[...] = jnp.broadcast_to(m_new, m_ref.shape)

    offset = carry[...]
    scores = []
    for pg in range(ppg):
        s = mm(q, k_refs[pg][...], NT) * scale - (offset + c_refs[pg][0])
        scores.append(jnp.where(same_head, s, NEG))
        offset = offset + tot_refs[pg][0]
    carry[...] = offset
    update(scores, [v_refs[pg][...] for pg in range(ppg)])

    @pl.when(g == pl.num_programs(1) - 1)
    def _():
        zpad = jnp.zeros((CHUNK - n_q, D_HEAD), F32)
        kn = jnp.concatenate([kn_ref[0], zpad], axis=0)
        vn = jnp.concatenate([vn_ref[0], zpad], axis=0)
        bias = carry[:, 0:CHUNK] + cn_ref[0]
        s = mm(q, kn, NT) * scale - bias
        r = lax.broadcasted_iota(jnp.int32, (n_q, CHUNK), 0)
        c = lax.broadcasted_iota(jnp.int32, (n_q, CHUNK), 1)
        ok = ((r % N_HEADS) == (c % N_HEADS)) & (c // N_HEADS <= r // N_HEADS) & (c < n_q)
        update([jnp.where(ok, s, NEG)], [vn])
        o_ref[0] = acc_ref[...] / l_ref[:, 0:1]


def _fox_sample(page_table, q, kn, vn, cn, cache_k, cache_v, c_page, tot_page, *, layer, n_pool, ppg, precise):
    n_seq, n_q, _ = q.shape
    n_pages = page_table.shape[1]
    n_keys = CHUNK * N_HEADS
    base = layer * n_pool
    pt = page_table.reshape(-1)

    def page(pg, ndim):
        return lambda n, g, pt_ref: (base + pt_ref[n * n_pages + g * ppg + pg],) + (0,) * (ndim - 1)

    seq3 = lambda n, g, pt_ref: (n, 0, 0)
    in_specs = [pl.BlockSpec((1, n_q, D_HEAD), seq3)] * 3 + [pl.BlockSpec((1, 1, CHUNK), seq3)]
    in_specs += [pl.BlockSpec((n_keys, D_HEAD), page(pg, 2)) for _ in range(2) for pg in range(ppg)]
    in_specs += [pl.BlockSpec((1, 1, n_keys), page(pg, 3)) for _ in range(2) for pg in range(ppg)]
    grid_spec = pltpu.PrefetchScalarGridSpec(
        num_scalar_prefetch=1,
        grid=(n_seq, n_pages // ppg),
        in_specs=in_specs,
        out_specs=pl.BlockSpec((1, n_q, D_HEAD), seq3),
        scratch_shapes=[pltpu.VMEM((n_q, LANES), F32), pltpu.VMEM((n_q, LANES), F32),
                        pltpu.VMEM((n_q, D_HEAD), F32), pltpu.VMEM((1, n_keys), F32)],
    )
    return pl.pallas_call(
        functools.partial(_fox_sample_kernel, ppg=ppg, n_q=n_q, precise=precise),
        grid_spec=grid_spec,
        out_shape=jax.ShapeDtypeStruct((n_seq, n_q, D_HEAD), F32),
        compiler_params=_params(2),
        name="fox_sample_split" if precise else "fox_sample",
    )(pt, q, kn, vn, cn, *([cache_k] * ppg), *([cache_v] * ppg), *([c_page] * ppg), *([tot_page] * ppg))


def _merge_kernel(oa_ref, ob_ref, oc_ref, gates_ref, x_ref, wb_ref, wo_ref, nf_ref, wr_ref, br_ref,
                  x1_ref, xn_ref, gate_ref, *, precise):
    mm = functools.partial(_dot, precise=precise)
    tm = x_ref.shape[0]
    merged = None
    for b, o_ref in enumerate((oa_ref, ob_ref, oc_ref)):
        term = mm(o_ref[...], wb_ref[b]) * _sigmoid(gates_ref[:, b * D_MODEL:(b + 1) * D_MODEL])
        merged = term if merged is None else merged + term
    x1 = x_ref[...] + mm(merged, wo_ref[...])
    x1_ref[...] = x1
    xn = _split2(_rms(x1, nf_ref[...]))
    xn_ref[...] = xn[0]
    logits = _dot(xn, wr_ref[...], precise=True) + br_ref[...]

    lane = lax.broadcasted_iota(jnp.int32, (tm, LANES), 1).astype(F32)
    big = float(LANES)
    is_grp = lane < float(N_GROUPS)
    mx = jnp.max(jnp.where(is_grp, logits, NEG), axis=-1, keepdims=True)
    p_grp = 1.0 / jnp.sum(jnp.where(is_grp, jnp.exp(jnp.where(is_grp, logits - mx, 0.0)), 0.0),
                          axis=-1, keepdims=True)
    g_star = jnp.min(jnp.where(is_grp & (logits == mx), lane, big), axis=-1, keepdims=True)
    base = float(LANE_EXPERT) + float(EXP_PER_GROUP) * g_star
    is_exp = (lane >= base) & (lane < base + float(EXP_PER_GROUP))
    le = jnp.where(is_exp, logits, NEG)
    t1 = jnp.max(le, axis=-1, keepdims=True)
    i1 = jnp.min(jnp.where(is_exp & (le == t1), lane, big), axis=-1, keepdims=True)
    le2 = jnp.where(lane == i1, NEG, le)
    t2 = jnp.max(le2, axis=-1, keepdims=True)
    i2 = jnp.min(jnp.where(is_exp & (lane != i1) & (le2 == t2), lane, big), axis=-1, keepdims=True)
    e2 = jnp.exp(t2 - t1)
    w1 = p_grp / (1.0 + e2)
    w2 = w1 * e2
    gate_ref[...] = jnp.where(lane == i1, w1, jnp.where(lane == i2, w2, 0.0))


def _merge(oa, ob, oc, p, x, wb, wo, nf, wr, br, tm, precise):
    r = x.shape[0]
    rows = lambda i: (i, 0)
    const2 = lambda i: (0, 0)
    return pl.pallas_call(
        functools.partial(_merge_kernel, precise=precise),
        grid=(r // tm,),
        in_specs=[
            pl.BlockSpec((tm, BRANCH_W), rows),
            pl.BlockSpec((tm, BRANCH_W), rows),
            pl.BlockSpec((tm, BRANCH_W), rows),
            pl.BlockSpec((tm, N_BRANCH * D_MODEL), rows),
            pl.BlockSpec((tm, D_MODEL), rows),
            pl.BlockSpec((N_BRANCH, BRANCH_W, D_MODEL), lambda i: (0, 0, 0)),
            pl.BlockSpec((D_MODEL, D_MODEL), const2),
            pl.BlockSpec((1, D_MODEL), const2),
            pl.BlockSpec((D_MODEL, LANES), const2),
            pl.BlockSpec((1, LANES), const2),
        ],
        out_specs=[
            pl.BlockSpec((tm, D_MODEL), rows),
            pl.BlockSpec((tm, D_MODEL), rows),
            pl.BlockSpec((tm, LANES), rows),
        ],
        out_shape=[jax.ShapeDtypeStruct((r, D_MODEL), F32), jax.ShapeDtypeStruct((r, D_MODEL), BF16),
                   jax.ShapeDtypeStruct((r, LANES), F32)],
        compiler_params=_params(1),
        name="merge_split" if precise else "merge",
    )(oa, ob, oc, p, x, wb, wo, nf, wr, br)


def _moe_kernel(xn_ref, gate_ref, x1_ref, w1_ref, w3_ref, w2_ref, nfin_ref, *out_refs, final):
    x2_ref = out_refs[0]
    e = pl.program_id(1)

    @pl.when(e == 0)
    def _():
        x2_ref[...] = x1_ref[...]

    xn = xn_ref[...]
    lane = lax.broadcasted_iota(jnp.int32, gate_ref.shape, 1)
    g_e = jnp.sum(jnp.where(lane == LANE_EXPERT + e, gate_ref[...], 0.0), axis=-1, keepdims=True)
    h = _silu(_dot(xn, w1_ref[0])) * _dot(xn, w3_ref[0])
    x2_ref[...] += _dot(h * g_e, w2_ref[0])

    if final:
        @pl.when(e == pl.num_programs(1) - 1)
        def _():
            out_refs[1][...] = _rms(x2_ref[...], nfin_ref[...])


def _moe(xn, gate, x1, w1, w3, w2, nfin, tm, final):
    r = x1.shape[0]
    rows = lambda i, e: (i, 0)
    out_specs = [pl.BlockSpec((tm, D_MODEL), rows)]
    out_shape = [jax.ShapeDtypeStruct((r, D_MODEL), F32)]
    if final:
        out_specs.append(pl.BlockSpec((tm, D_MODEL), rows))
        out_shape.append(jax.ShapeDtypeStruct((r, D_MODEL), F32))
    return pl.pallas_call(
        functools.partial(_moe_kernel, final=final),
        grid=(r // tm, N_EXPERTS),
        in_specs=[
            pl.BlockSpec((tm, D_MODEL), rows),
            pl.BlockSpec((tm, LANES), rows),
            pl.BlockSpec((tm, D_MODEL), rows),
            pl.BlockSpec((1, D_MODEL, D_FF), lambda i, e: (e, 0, 0)),
            pl.BlockSpec((1, D_MODEL, D_FF), lambda i, e: (e, 0, 0)),
            pl.BlockSpec((1, D_FF, D_MODEL), lambda i, e: (e, 0, 0)),
            pl.BlockSpec((1, D_MODEL), lambda i, e: (0, 0)),
        ],
        out_specs=out_specs,
        out_shape=out_shape,
        compiler_params=_params(2),
        name="moe",
    )(xn, gate, x1, w1, w3, w2, nfin)


def _rope_tables(pos):
    half = D_HEAD // 2
    inv = jnp.asarray((ROPE_BASE ** (-np.arange(half, dtype=np.float64) / half)).astype(np.float32))
    ang = pos.astype(F32)[:, None] * inv[None, :]
    cos, sin = jnp.cos(ang), jnp.sin(ang)
    return jnp.concatenate([cos, cos], axis=-1), jnp.concatenate([-sin, sin], axis=-1)


def _retention_tables(t_eff):
    idx = jnp.arange(CHUNK, dtype=F32)
    log_gamma = jnp.log(1.0 - jnp.exp2(-5.0 - jnp.arange(N_HEADS, dtype=F32)))
    incl = idx[:, None] >= idx[None, :]
    rel = jnp.where(incl, idx[:, None] - idx[None, :], 0.0)
    dmat = jnp.where(incl, jnp.exp(rel[None] * log_gamma[:, None, None]), 0.0)
    cross = jnp.exp((idx + 1.0)[:, None] * log_gamma[None, :])
    kdec = jnp.exp((t_eff - 1.0 - idx)[:, None] * log_gamma[None, :])
    tot = jnp.broadcast_to(jnp.exp(t_eff * log_gamma)[None, :], (CHUNK, N_HEADS))
    rtab = jnp.concatenate([cross, kdec, tot, jnp.zeros((CHUNK, LANES - 3 * N_HEADS), F32)], axis=1)
    return dmat, rtab


def _lanes_row(pairs):
    row = jnp.zeros((LANES,), F32)
    for start, vals in pairs:
        row = row.at[start:start + vals.shape[0]].set(vals.astype(F32))
    return row


def _pick(n, candidates):
    for c in candidates:
        if n % c == 0:
            return c
    raise ValueError(f"no tile in {candidates} divides {n}")


def kernel(x_prompt, x_sample, cache_k, cache_v, cache_logf, page_table, state_delta, state_conv, state_ret,
           meta_tokens, norm_mix, norm_ffn, norm_final, w_in, conv_w, a_log, dt_bias, a_norm, b_fbias, c_norm,
           w_branch, w_out, w_router_group, b_router_group, w_router_expert, b_router_expert, w1, w3, w2):
    nb, seq, _ = x_prompt.shape
    ns, ts, _ = x_sample.shape
    depth, n_pool = cache_k.shape[:2]
    n_pages = page_table.shape[1]
    assert seq % CHUNK == 0 and ts <= SAMPLE_ROWS and cache_k.shape[2] == CHUNK
    lp = CHUNK + seq
    n_chunks = lp // CHUNK
    rp = nb * lp
    rs = ns * SAMPLE_ROWS
    tm_p = _pick(rp, (512, 256, 128))
    tm_moe = _pick(rp, (768, 512, 256, 128))
    tm_s = _pick(rs, (256, 128, 64, 32, 16, 8))

    xp = jnp.concatenate([jnp.zeros((nb, PAD_FRONT, D_MODEL), F32),
                          jnp.broadcast_to(meta_tokens[None], (nb, N_META, D_MODEL)), x_prompt],
                         axis=1).reshape(rp, D_MODEL)
    xs = jnp.concatenate([x_sample, jnp.zeros((ns, SAMPLE_ROWS - ts, D_MODEL), F32)], axis=1).reshape(rs, D_MODEL)

    cos_p, sin_p = _rope_tables(jnp.arange(lp) - PAD_FRONT)
    cos_s, sin_s = _rope_tables(n_pages * CHUNK + jnp.arange(CHUNK))
    dmat_p, rtab_p = _retention_tables(float(CHUNK))
    dmat_s, rtab_s = _retention_tables(float(ts))

    assert (ts * N_HEADS) % 8 == 0
    n_cache = depth * n_pool
    cache_k2 = cache_k.reshape(n_cache * CHUNK * N_HEADS, D_HEAD)
    cache_v2 = cache_v.reshape(n_cache * CHUNK * N_HEADS, D_HEAD)
    c_page, tot_page = _page_cumsum(cache_logf.reshape(n_cache, CHUNK * N_HEADS))
    c_page = c_page.reshape(n_cache, 1, CHUNK * N_HEADS)
    tot_page = tot_page.reshape(n_cache, 1, CHUNK * N_HEADS)
    tb = _pick(lp, (384, 256, 128))
    par_p = _pick(nb, (PAR_PROMPT, 1))
    par_s = _pick(ns, (PAR_SAMPLE, 2, 1))
    ppg = _pick(n_pages, (8, 4, 2, 1))

    def sample_heads(a):
        return a.reshape(ns, SAMPLE_ROWS, N_HEADS, D_HEAD)[:, :ts].reshape(ns, ts * N_HEADS, D_HEAD)

    def prompt_rows(a):
        return a.reshape(nb, lp, a.shape[-1])[:, PAD_FRONT:]

    def sample_rows(a):
        return a.reshape(ns, SAMPLE_ROWS, a.shape[-1])[:, :ts]

    def col(a, c, w=BRANCH_W):
        return a[:, c * BRANCH_W:c * BRANCH_W + w]

    outs_p = [[] for _ in range(6)]
    outs_s = [[] for _ in range(6)]
    yp = ys = None
    for l in range(depth):
        final = l == depth - 1
        hp = not final
        wdt = F32 if hp else BF16
        w_main, w_small = _reorder_w_in(w_in, l, wdt)
        g_mix = norm_mix[l][None]
        p_p, ps_p = _proj(xp, g_mix, w_main, w_small, tm_p, 512, hp)
        p_s, ps_s = _proj(xs, g_mix, w_main, w_small, tm_s, 512, hp)

        avec = jnp.zeros((8, LANES), F32)
        avec = avec.at[0].set(_lanes_row([(LANE_ALPHA, a_log[l])]))
        avec = avec.at[1].set(_lanes_row([(LANE_ALPHA, dt_bias[l])]))
        avec = avec.at[2].set(_lanes_row([(LANE_F, b_fbias[l])]))
        shared = (conv_w[l], avec, a_norm[l][None], c_norm[l][None])
        oa_p, oc_p, lf_p, cc_p, sd_p, sr_p = _seq(
            p_p, ps_p, cos_p, sin_p, *shared, dmat_p, rtab_p, n_seq=nb, n_chunks=n_chunks, tc_in=CHUNK,
            valid_hi=CHUNK, front_pad=PAD_FRONT, precise=hp, n_par=par_p)
        cb0 = jnp.pad(state_conv[l], ((0, 0), (8 - (CONV_W - 1), 0), (0, 0)))
        oa_s, oc_s, lf_s, cc_s, sd_s, sr_s = _seq(
            p_s, ps_s, cos_s, sin_s, *shared, dmat_s, rtab_s, n_seq=ns, n_chunks=1, tc_in=SAMPLE_ROWS,
            valid_hi=ts, front_pad=0, precise=hp, n_par=par_s, init=(state_delta[l], state_ret[l], cb0))

        ct = jnp.pad(jnp.swapaxes(cc_p[:, LANE_F:LANE_F + N_HEADS].reshape(nb, lp, N_HEADS), 1, 2),
                     ((0, 0), (0, 8 - N_HEADS), (0, 0)))
        ob_p = _fox_prompt(p_p, ct, nb, lp, tb, hp)
        cn = cc_s[:, LANE_F:LANE_F + N_HEADS].reshape(ns, SAMPLE_ROWS, N_HEADS)[:, :ts].reshape(ns, 1, ts * N_HEADS)
        cn = jnp.pad(cn, ((0, 0), (0, 0), (0, CHUNK - ts * N_HEADS)))
        o_s = _fox_sample(page_table, sample_heads(col(p_s, COL_BQ)), sample_heads(col(p_s, COL_BK)),
                          sample_heads(col(p_s, COL_BV)), cn, cache_k2, cache_v2, c_page, tot_page,
                          layer=l, n_pool=n_pool, ppg=ppg, precise=hp)
        ob_s = jnp.pad(o_s.reshape(ns, ts, BRANCH_W), ((0, 0), (0, SAMPLE_ROWS - ts), (0, 0))).reshape(rs, BRANCH_W)

        wr = jnp.concatenate([w_router_group[l], w_router_expert[l],
                              jnp.zeros((D_MODEL, LANES - N_GROUPS - N_EXPERTS), F32)], axis=1)
        br = _lanes_row([(LANE_GROUP, b_router_group[l]), (LANE_EXPERT, b_router_expert[l].reshape(-1))])[None]
        nf = norm_ffn[l][None]
        wb, wo = w_branch[l].astype(wdt), w_out[l].astype(wdt)
        x1_p, xn_p, gate_p = _merge(oa_p, ob_p, oc_p, p_p, xp, wb, wo, nf, wr, br, min(tm_p, 256), hp)
        x1_s, xn_s, gate_s = _merge(oa_s, ob_s, oc_s, p_s, xs, wb, wo, nf, wr, br, tm_s, hp)

        ew1 = w1[l].reshape(N_EXPERTS, D_MODEL, D_FF).astype(BF16)
        ew3 = w3[l].reshape(N_EXPERTS, D_MODEL, D_FF).astype(BF16)
        ew2 = w2[l].reshape(N_EXPERTS, D_FF, D_MODEL).astype(BF16)
        res_p = _moe(xn_p, gate_p, x1_p, ew1, ew3, ew2, norm_final[None], tm_moe, final)
        res_s = _moe(xn_s, gate_s, x1_s, ew1, ew3, ew2, norm_final[None], tm_s, final)
        xp, xs = res_p[0], res_s[0]
        if final:
            yp, ys = res_p[1], res_s[1]

        for lst, rows_of, p, lf_a, sd, sr in ((outs_p, prompt_rows, p_p, lf_p, sd_p, sr_p),
                                              (outs_s, sample_rows, p_s, lf_s, sd_s, sr_s)):
            k_new = rows_of(col(p, COL_BK))
            v_new = rows_of(col(p, COL_BV))
            lst[0].append(k_new.reshape(k_new.shape[:2] + (N_HEADS, D_HEAD)))
            lst[1].append(v_new.reshape(v_new.shape[:2] + (N_HEADS, D_HEAD)))
            lst[2].append(rows_of(lf_a[:, LANE_F:LANE_F + N_HEADS]))
            lst[3].append(sd)
            lst[4].append(rows_of(col(p, COL_AQKV, A_QKV))[:, -(CONV_W - 1):])
            lst[5].append(sr)

    y_prompt = yp.reshape(nb, lp, D_MODEL)[:, CHUNK:]
    y_sample = ys.reshape(ns, SAMPLE_ROWS, D_MODEL)[:, :ts]
    return (y_prompt, y_sample, *(jnp.stack(a) for a in outs_p), *(jnp.stack(a) for a in outs_s))
```

```python
import functools

import jax
import jax.numpy as jnp
import numpy as np
from jax import lax
from jax.experimental import pallas as pl
from jax.experimental.pallas import tpu as pltpu

F32 = jnp.float32
BF16 = jnp.bfloat16

D_MODEL = 1024
N_META = 16
CHUNK = 128
PAD_FRONT = CHUNK - N_META
N_HEADS = 4
D_HEAD = 128
BRANCH_W = N_HEADS * D_HEAD
A_QKV = 3 * BRANCH_W
CONV_W = 4
N_BRANCH = 3
N_GROUPS = 4
EXP_PER_GROUP = 4
N_EXPERTS = N_GROUPS * EXP_PER_GROUP
D_FF = 512
ROPE_BASE = 10000.0
EPS = 1e-6
NEG = -1e30
SAMPLE_ROWS = 8
LANES = 128

COL_GATES = 0
COL_AQKV = 6
COL_AZ = 9
COL_BQ, COL_BK, COL_BV = 10, 11, 12
COL_CQ, COL_CK, COL_CV, COL_CG = 13, 14, 15, 16
N_MAIN = 17 * BRANCH_W
LANE_BETA, LANE_ALPHA, LANE_F = 0, 4, 8
LANE_GROUP, LANE_EXPERT = 0, 4

VMEM_LIMIT = 56 * 1024 * 1024
PAR_PROMPT, PAR_SAMPLE = 2, 4

NN = (((1,), (0,)), ((), ()))
NT = (((1,), (1,)), ((), ()))
TN = (((0,), (0,)), ((), ()))


def _split2(x):
    hi = x.astype(BF16)
    return hi, (x - hi.astype(F32)).astype(BF16)


def _dot(a, b, dims=NN, precise=False):
    def dg(x, y):
        return lax.dot_general(x, y, dims, preferred_element_type=F32)

    if not precise:
        a = a[0] if isinstance(a, tuple) else a.astype(BF16)
        b = b[0] if isinstance(b, tuple) else b.astype(BF16)
        return dg(a, b)
    a_hi, a_lo = a if isinstance(a, tuple) else _split2(a)
    b_hi, b_lo = b if isinstance(b, tuple) else _split2(b)
    return dg(a_hi, b_hi) + dg(a_hi, b_lo) + dg(a_lo, b_hi)


def _split3(x):
    x1 = x.astype(BF16)
    r1 = x - x1.astype(F32)
    x2 = r1.astype(BF16)
    x3 = (r1 - x2.astype(F32)).astype(BF16)
    return x1, x2, x3


def _sigmoid(x):
    return 1.0 / (1.0 + jnp.exp(-x))


def _silu(x):
    return x * _sigmoid(x)


def _softplus(x):
    return jnp.maximum(x, 0.0) + jnp.log1p(jnp.exp(-jnp.abs(x)))


def _rms(x, g):
    return x * lax.rsqrt(jnp.mean(x * x, axis=-1, keepdims=True) + EPS) * g


def _params(n_axes):
    return pltpu.CompilerParams(dimension_semantics=("arbitrary",) * n_axes, vmem_limit_bytes=VMEM_LIMIT)


SEGMENTS = ((0, A_QKV + 8 * BRANCH_W + 12, N_BRANCH * D_MODEL),
            (N_BRANCH * D_MODEL, 0, A_QKV + BRANCH_W),
            (N_BRANCH * D_MODEL + 4 * BRANCH_W, A_QKV + BRANCH_W + 8, 3 * BRANCH_W),
            (N_BRANCH * D_MODEL + 7 * BRANCH_W, A_QKV + 4 * BRANCH_W + 12, 4 * BRANCH_W))
OLD_SMALL_A = A_QKV + BRANCH_W
OLD_SMALL_F = A_QKV + 4 * BRANCH_W + 8


def _reorder_kernel(blk_ref, sh_ref, a_ref, b_ref, o_ref, *, shifts):
    j = pl.program_id(0)
    for shift in shifts:
        @pl.when(sh_ref[j] == shift)
        def _(shift=shift):
            a = a_ref[...]
            if shift:
                a = jnp.concatenate([a[:, shift:], b_ref[:, :shift]], axis=1)
            o_ref[...] = a.astype(o_ref.dtype)


def _small_kernel(a_ref, f_ref, o_ref):
    lane = lax.broadcasted_iota(jnp.int32, a_ref.shape, 1)
    o_ref[...] = jnp.where(lane < LANE_F, a_ref[...], jnp.where(lane < LANE_F + N_HEADS, f_ref[...], 0.0)
                           ).astype(o_ref.dtype)


def _reorder_w_in(w_in, layer, dtype):
    n_old = w_in.shape[2]
    last = (n_old - 1) // LANES
    blks, shifts = [], []
    for new0, old0, width in SEGMENTS:
        assert new0 % LANES == 0 and width % LANES == 0
        for t in range(width // LANES):
            blks.append(old0 // LANES + t)
            shifts.append(old0 % LANES)
    grid_spec = pltpu.PrefetchScalarGridSpec(
        num_scalar_prefetch=2,
        grid=(len(blks),),
        in_specs=[pl.BlockSpec((None, D_MODEL, LANES), lambda j, blk, sh: (layer, 0, blk[j])),
                  pl.BlockSpec((None, D_MODEL, LANES), lambda j, blk, sh: (layer, 0, jnp.minimum(blk[j] + 1, last)))],
        out_specs=pl.BlockSpec((D_MODEL, LANES), lambda j, blk, sh: (0, j)),
    )
    w_main = pl.pallas_call(
        functools.partial(_reorder_kernel, shifts=tuple(sorted(set(shifts)))),
        grid_spec=grid_spec,
        out_shape=jax.ShapeDtypeStruct((D_MODEL, N_MAIN), dtype),
        compiler_params=_params(1),
        name="reorder_w_in",
    )(jnp.asarray(blks, jnp.int32), jnp.asarray(shifts, jnp.int32), w_in, w_in)
    assert OLD_SMALL_A % LANES == LANE_BETA and OLD_SMALL_F % LANES == LANE_F
    w_small = pl.pallas_call(
        _small_kernel,
        grid=(1,),
        in_specs=[pl.BlockSpec((None, D_MODEL, LANES), lambda i: (layer, 0, OLD_SMALL_A // LANES)),
                  pl.BlockSpec((None, D_MODEL, LANES), lambda i: (layer, 0, OLD_SMALL_F // LANES))],
        out_specs=pl.BlockSpec((D_MODEL, LANES), lambda i: (0, 0)),
        out_shape=jax.ShapeDtypeStruct((D_MODEL, LANES), dtype),
        compiler_params=_params(1),
        name="reorder_w_small",
    )(w_in, w_in)
    return w_main, w_small


def _proj_kernel(x_ref, g_ref, w_ref, ws_ref, p_ref, ps_ref, xn_ref, *, tn, precise):
    j = pl.program_id(1)

    @pl.when(j == 0)
    def _():
        xn = _split2(_rms(x_ref[...], g_ref[...]))
        xn_ref[0] = xn[0]
        if precise:
            xn_ref[1] = xn[1]
        ps_ref[...] = _dot(xn, ws_ref[...], precise=precise)

    if precise:
        p_ref[...] = _dot((xn_ref[0], xn_ref[1]), w_ref[...], precise=True).astype(p_ref.dtype)
    else:
        p_ref[...] = _dot(xn_ref[0], w_ref[:, pl.ds(pl.multiple_of(j * tn, tn), tn)]).astype(p_ref.dtype)


def _proj(x, g, w_main, w_small, tm, tn, precise, out_dtype=F32):
    r = x.shape[0]
    w_spec = (pl.BlockSpec((D_MODEL, tn), lambda i, j: (0, j)) if precise else
              pl.BlockSpec((D_MODEL, N_MAIN), lambda i, j: (0, 0)))
    return pl.pallas_call(
        functools.partial(_proj_kernel, tn=tn, precise=precise),
        grid=(r // tm, N_MAIN // tn),
        in_specs=[pl.BlockSpec((tm, D_MODEL), lambda i, j: (i, 0)),
                  pl.BlockSpec((1, D_MODEL), lambda i, j: (0, 0)),
                  w_spec,
                  pl.BlockSpec((D_MODEL, LANES), lambda i, j: (0, 0))],
        out_specs=[pl.BlockSpec((tm, tn), lambda i, j: (i, j)),
                   pl.BlockSpec((tm, LANES), lambda i, j: (i, 0))],
        out_shape=[jax.ShapeDtypeStruct((r, N_MAIN), out_dtype), jax.ShapeDtypeStruct((r, LANES), F32)],
        scratch_shapes=[pltpu.VMEM((2 if precise else 1, tm, D_MODEL), BF16)],
        compiler_params=_params(2),
        name="proj_split" if precise else "proj",
    )(x, g, w_main, w_small)


def _delta_heads(qs, ks, vs, betas, gcols, grows, states, incl, strict, precise):
    mm = functools.partial(_dot, precise=precise)
    n = range(len(qs))
    egs = [jnp.exp(gcols[i]) for i in n]
    decays = [jnp.where(incl, jnp.exp(jnp.where(incl, gcols[i] - grows[i], 0.0)), 0.0) for i in n]
    kbs = [ks[i] * betas[i] for i in n]
    ms = [jnp.where(strict, mm(kbs[i], ks[i], NT) * decays[i], 0.0) for i in n]
    rs = [-m for m in ms]
    pows = [_dot(r, r) for r in rs]
    for it in range(6):
        rps = [_dot(rs[i], pows[i]) for i in n]
        if it < 5:
            nxt = [_dot(pows[i], pows[i]) for i in n]
        rs = [rs[i] + pows[i] + rps[i] for i in n]
        if it < 5:
            pows = nxt
    for _ in range(2 if precise else 1):
        es = [-(ms[i] + rs[i] + _dot(ms[i], rs[i], precise=True)) for i in n]
        res = [_dot(rs[i], es[i]) for i in n]
        rs = [rs[i] + es[i] + res[i] for i in n]
    vbs = [vs[i] * betas[i] for i in n]
    kbes = [kbs[i] * egs[i] for i in n]
    us = [vbs[i] + mm(rs[i], vbs[i]) for i in n]
    ws = [kbes[i] + mm(rs[i], kbes[i]) for i in n]
    attns = [mm(qs[i], ks[i], NT) * decays[i] for i in n]
    v_news = [us[i] - mm(ws[i], states[i]) for i in n]
    outs = [mm(qs[i] * egs[i], states[i]) + mm(attns[i], v_news[i]) for i in n]
    g_lasts = [g[CHUNK - 1:CHUNK, :] for g in gcols]
    new_states = [states[i] * jnp.exp(g_lasts[i]) + mm(ks[i] * jnp.exp(g_lasts[i] - gcols[i]), v_news[i], TN)
                  for i in n]
    return outs, new_states


def _seq_kernel(*refs, n_par, tc_in, valid_hi, front_pad, has_init, precise):
    if has_init:
        (aqkv_ref, az_ref, cq_ref, ck_ref, cv_ref, cg_ref, sm_ref, cos_ref, sin_ref,
         convw_ref, avec_ref, anorm_ref, cnorm_ref, dmat_ref, rtab_ref,
         sd0_ref, sr0_ref, cb0_ref,
         oa_ref, oc_ref, lf_ref, cc_ref, sd_ref, sr_ref, xbuf, cbuf, carry) = refs
    else:
        (aqkv_ref, az_ref, cq_ref, ck_ref, cv_ref, cg_ref, sm_ref, cos_ref, sin_ref,
         convw_ref, avec_ref, anorm_ref, cnorm_ref, dmat_ref, rtab_ref,
         oa_ref, oc_ref, lf_ref, cc_ref, sd_ref, sr_ref, xbuf, cbuf, carry) = refs
    mm = functools.partial(_dot, precise=precise)
    c = pl.program_id(1)

    def ext(ref, s, cols=None):
        v = (ref[s] if cols is None else ref[s, :, cols]).astype(F32)
        if tc_in == CHUNK:
            return v
        return jnp.concatenate([v, jnp.zeros((CHUNK - tc_in, v.shape[1]), v.dtype)], axis=0)

    @pl.when(c == 0)
    def _():
        carry[...] = jnp.zeros_like(carry)
        if has_init:
            xbuf[:, 0:8, :] = cb0_ref[...]
            sd_ref[...] = sd0_ref[...]
            sr_ref[...] = sr0_ref[...]
        else:
            xbuf[:, 0:8, :] = jnp.zeros((n_par, 8, A_QKV), F32)
            sd_ref[...] = jnp.zeros_like(sd_ref)
            sr_ref[...] = jnp.zeros_like(sr_ref)

    rows = lax.broadcasted_iota(jnp.int32, (CHUNK, LANES), 0)
    lane = lax.broadcasted_iota(jnp.int32, (CHUNK, LANES), 1)
    lo = jnp.where(c == 0, front_pad, 0)
    valid = (rows >= lo) & (rows < valid_hi)
    incl = rows >= lane
    strict = rows > lane
    is_g = (lane >= LANE_ALPHA) & (lane < LANE_ALPHA + N_HEADS)
    is_f = (lane >= LANE_F) & (lane < LANE_F + N_HEADS)
    tril = jnp.where(incl, 1.0, 0.0).astype(BF16)
    av = avec_ref[...]
    cos2 = cos_ref[...]
    sin2 = sin_ref[...]
    rtab = rtab_ref[...]

    heads = range(N_HEADS)
    hsl = [slice(h * D_HEAD, (h + 1) * D_HEAD) for h in heads]
    chains = [(s, h) for s in range(n_par) for h in heads]
    qs, ks, vs, betas, gcols, grows = [], [], [], [], [], []
    for s in range(n_par):
        for j in range(A_QKV // LANES):
            cols = slice(j * LANES, (j + 1) * LANES)
            xbuf[s, 8:8 + CHUNK, cols] = ext(aqkv_ref, s, cols)
            acc = xbuf[s, 5:5 + CHUNK, cols] * convw_ref[0:1, cols]
            for i in range(1, CONV_W):
                acc = acc + xbuf[s, 5 + i:5 + i + CHUNK, cols] * convw_ref[i:i + 1, cols]
            cbuf[s, :, cols] = jnp.where(valid, _silu(acc), 0.0)
            xbuf[s, 0:8, cols] = xbuf[s, CHUNK:CHUNK + 8, cols]

        sm = ext(sm_ref, s)
        g_all = -jnp.exp(av[0:1, :]) * _softplus(sm + av[1:2, :])
        lf_all = -_softplus(-(sm + av[2:3, :]))
        z = jnp.where(valid & is_g, g_all, jnp.where(valid & is_f, lf_all, 0.0))
        cum = sum(jnp.dot(tril, part, preferred_element_type=F32) for part in _split3(z))
        cum_t = cum.T
        cc = cum + carry[s]
        carry[s] = cc[CHUNK - 1:CHUNK, :]
        lf_ref[s] = z[:tc_in]
        cc_ref[s] = cc[:tc_in]

        beta_all = _sigmoid(sm)
        for h in heads:
            q = cbuf[s, :, hsl[h]]
            k = cbuf[s, :, BRANCH_W + h * D_HEAD:BRANCH_W + (h + 1) * D_HEAD]
            qs.append(q * lax.rsqrt(jnp.sum(q * q, axis=-1, keepdims=True) + EPS) * (D_HEAD ** -0.5))
            ks.append(k * lax.rsqrt(jnp.sum(k * k, axis=-1, keepdims=True) + EPS))
            vs.append(cbuf[s, :, 2 * BRANCH_W + h * D_HEAD:2 * BRANCH_W + (h + 1) * D_HEAD])
            betas.append(beta_all[:, LANE_BETA + h:LANE_BETA + h + 1])
            gcols.append(cum[:, LANE_ALPHA + h:LANE_ALPHA + h + 1])
            grows.append(cum_t[LANE_ALPHA + h:LANE_ALPHA + h + 1, :])

    outs, new_states = _delta_heads(qs, ks, vs, betas, gcols, grows, [sd_ref[s, h] for s, h in chains],
                                    incl, strict, precise)
    for i, (s, h) in enumerate(chains):
        sd_ref[s, h] = new_states[i]
        oa = _rms(outs[i], anorm_ref[...]) * _silu(ext(az_ref, s, hsl[h]))
        oa_ref[s, :, hsl[h]] = oa[:tc_in]

    qcs = [ext(cq_ref, s, hsl[h]) for s, h in chains]
    kcs = [ext(ck_ref, s, hsl[h]) for s, h in chains]
    vcs = [ext(cv_ref, s, hsl[h]) for s, h in chains]
    qcs = [q * cos2 + pltpu.roll(q, D_HEAD // 2, 1) * sin2 for q in qcs]
    kcs = [(k * cos2 + pltpu.roll(k, D_HEAD // 2, 1) * sin2) * (D_HEAD ** -0.5) for k in kcs]
    srs = [sr_ref[s, h] for s, h in chains]
    scores = [mm(qcs[i], kcs[i], NT) * dmat_ref[h] for i, (s, h) in enumerate(chains)]
    crosses = [mm(qcs[i] * rtab[:, h:h + 1], srs[i]) for i, (s, h) in enumerate(chains)]
    kvs = [mm(kcs[i] * rtab[:, 4 + h:5 + h], vcs[i], TN) for i, (s, h) in enumerate(chains)]
    inners = [mm(scores[i], vcs[i]) for i in range(len(chains))]
    for i, (s, h) in enumerate(chains):
        sr_ref[s, h] = srs[i] * rtab[0:1, 8 + h:9 + h] + kvs[i]
        oc = _rms(inners[i] + crosses[i], cnorm_ref[...]) * _silu(ext(cg_ref, s, hsl[h]))
        oc_ref[s, :, hsl[h]] = oc[:tc_in]


def _seq(p, ps, cos2, sin2, convw, avec, anorm, cnorm, dmat, rtab, *, n_seq, n_chunks, tc_in,
         valid_hi, front_pad, precise, n_par, init=None):
    r = p.shape[0]
    lp = n_chunks * tc_in
    p3 = p.reshape(n_seq, lp, N_MAIN)
    ps3 = ps.reshape(n_seq, lp, LANES)

    def rows(width, col):
        return pl.BlockSpec((n_par, tc_in, width), lambda n, c: (n, c, col))

    const2 = lambda n, c: (0, 0)
    state_spec = pl.BlockSpec((n_par, N_HEADS, D_HEAD, D_HEAD), lambda n, c: (n, 0, 0, 0))
    in_specs = [
        rows(A_QKV, COL_AQKV // 3),
        rows(BRANCH_W, COL_AZ),
        rows(BRANCH_W, COL_CQ),
        rows(BRANCH_W, COL_CK),
        rows(BRANCH_W, COL_CV),
        rows(BRANCH_W, COL_CG),
        rows(LANES, 0),
        pl.BlockSpec((CHUNK, D_HEAD), lambda n, c: (c, 0)),
        pl.BlockSpec((CHUNK, D_HEAD), lambda n, c: (c, 0)),
        pl.BlockSpec((CONV_W, A_QKV), const2),
        pl.BlockSpec((8, LANES), const2),
        pl.BlockSpec((1, D_HEAD), const2),
        pl.BlockSpec((1, D_HEAD), const2),
        pl.BlockSpec((N_HEADS, CHUNK, CHUNK), lambda n, c: (0, 0, 0)),
        pl.BlockSpec((CHUNK, LANES), const2),
    ]
    args = [p3, p3, p3, p3, p3, p3, ps3, cos2, sin2, convw, avec, anorm, cnorm, dmat, rtab]
    if init is not None:
        in_specs += [state_spec, state_spec, pl.BlockSpec((n_par, 8, A_QKV), lambda n, c: (n, 0, 0))]
        args += list(init)
    out_specs = [rows(BRANCH_W, 0), rows(BRANCH_W, 0), rows(LANES, 0), rows(LANES, 0), state_spec, state_spec]
    out_shape = [
        jax.ShapeDtypeStruct((n_seq, lp, BRANCH_W), F32),
        jax.ShapeDtypeStruct((n_seq, lp, BRANCH_W), F32),
        jax.ShapeDtypeStruct((n_seq, lp, LANES), F32),
        jax.ShapeDtypeStruct((n_seq, lp, LANES), F32),
        jax.ShapeDtypeStruct((n_seq, N_HEADS, D_HEAD, D_HEAD), F32),
        jax.ShapeDtypeStruct((n_seq, N_HEADS, D_HEAD, D_HEAD), F32),
    ]
    kern = functools.partial(_seq_kernel, n_par=n_par, tc_in=tc_in, valid_hi=valid_hi, front_pad=front_pad,
                             has_init=init is not None, precise=precise)
    oa, oc, lf, cc, sd, sr = pl.pallas_call(
        kern,
        grid=(n_seq // n_par, n_chunks),
        in_specs=in_specs,
        out_specs=out_specs,
        out_shape=out_shape,
        scratch_shapes=[pltpu.VMEM((n_par, CHUNK + 8, A_QKV), F32), pltpu.VMEM((n_par, CHUNK, A_QKV), F32),
                        pltpu.VMEM((n_par, 1, LANES), F32)],
        compiler_params=_params(2),
        name=("seq_prompt" if init is None else "seq_sample") + ("_split" if precise else ""),
    )(*args)
    return (oa.reshape(r, BRANCH_W), oc.reshape(r, BRANCH_W), lf.reshape(r, LANES), cc.reshape(r, LANES), sd, sr)


def _fox_prompt_kernel(q_ref, k_ref, v_ref, ct_ref, o_ref, m_ref, l_ref, acc_ref, *, tb, precise):
    mm = functools.partial(_dot, precise=precise)
    i = pl.program_id(1)
    j = pl.program_id(2)

    @pl.when(j == 0)
    def _():
        m_ref[...] = jnp.full_like(m_ref, NEG)
        l_ref[...] = jnp.zeros_like(l_ref)
        acc_ref[...] = jnp.zeros_like(acc_ref)

    def block(masked):
        if masked:
            qpos = i * tb + lax.broadcasted_iota(jnp.int32, (tb, tb), 0)
            kpos = j * tb + lax.broadcasted_iota(jnp.int32, (tb, tb), 1)
            ok = (kpos <= qpos) & (kpos >= PAD_FRONT)
        for h in range(N_HEADS):
            hs = slice(h * D_HEAD, (h + 1) * D_HEAD)
            s = mm(q_ref[:, hs].astype(F32) * (D_HEAD ** -0.5), k_ref[:, hs], NT) - ct_ref[0, h:h + 1, :]
            if masked:
                s = jnp.where(ok, s, NEG)
            m_old = m_ref[h][:, 0:1]
            m_new = jnp.maximum(m_old, jnp.max(s, axis=-1, keepdims=True))
            alpha = jnp.exp(m_old - m_new)
            p = jnp.exp(s - m_new)
            l_ref[h] = alpha * l_ref[h] + jnp.sum(p, axis=-1, keepdims=True)
            acc_ref[h] = alpha * acc_ref[h] + mm(p, v_ref[:, hs])
            m_ref[h] = jnp.broadcast_to(m_new, (tb, LANES))

    needs_mask = (j == i) | (j == 0)
    pl.when(needs_mask)(functools.partial(block, True))
    pl.when((j < i) & (j > 0))(functools.partial(block, False))

    @pl.when(j == i)
    def _():
        qvalid = (i * tb + lax.broadcasted_iota(jnp.int32, (tb, D_HEAD), 0)) >= PAD_FRONT
        for h in range(N_HEADS):
            o = acc_ref[h] / l_ref[h][:, 0:1]
            o_ref[:, h * D_HEAD:(h + 1) * D_HEAD] = jnp.where(qvalid, o, 0.0)


def _fox_prompt(p, ct, n_seq, lp, tb, precise):
    r = p.shape[0]
    nb = lp // tb
    return pl.pallas_call(
        functools.partial(_fox_prompt_kernel, tb=tb, precise=precise),
        grid=(n_seq, nb, nb),
        in_specs=[
            pl.BlockSpec((tb, BRANCH_W), lambda n, i, j: (n * nb + i, COL_BQ)),
            pl.BlockSpec((tb, BRANCH_W), lambda n, i, j: (n * nb + jnp.minimum(i, j), COL_BK)),
            pl.BlockSpec((tb, BRANCH_W), lambda n, i, j: (n * nb + jnp.minimum(i, j), COL_BV)),
            pl.BlockSpec((1, 8, tb), lambda n, i, j: (n, 0, jnp.minimum(i, j))),
        ],
        out_specs=pl.BlockSpec((tb, BRANCH_W), lambda n, i, j: (n * nb + i, 0)),
        out_shape=jax.ShapeDtypeStruct((r, BRANCH_W), F32),
        scratch_shapes=[pltpu.VMEM((N_HEADS, tb, LANES), F32), pltpu.VMEM((N_HEADS, tb, LANES), F32),
                        pltpu.VMEM((N_HEADS, tb, D_HEAD), F32)],
        compiler_params=_params(3),
        name="fox_prompt",
    )(p, p, p, ct)


def _page_cumsum_kernel(lf_ref, c_ref, tot_ref):
    n = CHUNK * N_HEADS
    ri = lax.broadcasted_iota(jnp.int32, (n, n), 0)
    ci = lax.broadcasted_iota(jnp.int32, (n, n), 1)
    same_head = (ri % N_HEADS) == (ci % N_HEADS)
    upper = jnp.where(same_head & (ri <= ci), 1.0, 0.0).astype(BF16)
    total = jnp.where(same_head, 1.0, 0.0).astype(BF16)
    parts = _split3(lf_ref[...])
    c_ref[...] = sum(jnp.dot(x, upper, preferred_element_type=F32) for x in parts)
    tot_ref[...] = sum(jnp.dot(x, total, preferred_element_type=F32) for x in parts)


def _page_cumsum(lf):
    n_pages, n = lf.shape
    tp = _pick(n_pages, (256, 128, 64, 32, 16, 8, n_pages))
    spec = pl.BlockSpec((tp, n), lambda i: (i, 0))
    return pl.pallas_call(
        _page_cumsum_kernel,
        grid=(n_pages // tp,),
        in_specs=[spec],
        out_specs=[spec, spec],
        out_shape=[jax.ShapeDtypeStruct((n_pages, n), F32)] * 2,
        compiler_params=_params(1),
        name="page_cumsum",
    )(lf)


def _fox_sample_kernel(*refs, ppg, n_q, precise):
    q_ref, kn_ref, vn_ref, cn_ref = refs[1:5]
    k_refs = refs[5:5 + ppg]
    v_refs = refs[5 + ppg:5 + 2 * ppg]
    c_refs = refs[5 + 2 * ppg:5 + 3 * ppg]
    tot_refs = refs[5 + 3 * ppg:5 + 4 * ppg]
    o_ref, m_ref, l_ref, acc_ref, carry = refs[5 + 4 * ppg:]
    mm = functools.partial(_dot, precise=precise)
    g = pl.program_id(1)
    n_keys = CHUNK * N_HEADS

    @pl.when(g == 0)
    def _():
        m_ref[...] = jnp.full_like(m_ref, NEG)
        l_ref[...] = jnp.zeros_like(l_ref)
        acc_ref[...] = jnp.zeros_like(acc_ref)
        carry[...] = jnp.zeros_like(carry)

    row = lax.broadcasted_iota(jnp.int32, (n_q, n_keys), 0)
    lane = lax.broadcasted_iota(jnp.int32, (n_q, n_keys), 1)
    same_head = (row % N_HEADS) == (lane % N_HEADS)
    scale = D_HEAD ** -0.5
    q = _split2(q_ref[0])

    def update(scores, values):
        top = scores[0]
        for s in scores[1:]:
            top = jnp.maximum(top, s)
        m_old = m_ref[:, 0:1]
        m_new = jnp.maximum(m_old, jnp.max(top, axis=-1, keepdims=True))
        alpha = jnp.exp(m_old - m_new)
        probs = [jnp.exp(s - m_new) for s in scores]
        mass = probs[0]
        for p in probs[1:]:
            mass = mass + p
        pv = mm(probs[0], values[0])
        for p, v in zip(probs[1:], values[1:]):
            pv = pv + mm(p, v)
        l_ref[...] = alpha * l_ref[...] + jnp.sum(mass, axis=-1, keepdims=True)
        acc_ref[...] = alpha * acc_ref[...] + pv
        m_ref[...] = jnp.broadcast_to(m_new, m_ref.shape)

    offset = carry[...]
    scores = []
    for pg in range(ppg):
        s = mm(q, k_refs[pg][...], NT) * scale - (offset + c_refs[pg][0])
        scores.append(jnp.where(same_head, s, NEG))
        offset = offset + tot_refs[pg][0]
    carry[...] = offset
    update(scores, [v_refs[pg][...] for pg in range(ppg)])

    @pl.when(g == pl.num_programs(1) - 1)
    def _():
        zpad = jnp.zeros((CHUNK - n_q, D_HEAD), F32)
        kn = jnp.concatenate([kn_ref[0], zpad], axis=0)
        vn = jnp.concatenate([vn_ref[0], zpad], axis=0)
        bias = carry[:, 0:CHUNK] + cn_ref[0]
        s = mm(q, kn, NT) * scale - bias
        r = lax.broadcasted_iota(jnp.int32, (n_q, CHUNK), 0)
        c = lax.broadcasted_iota(jnp.int32, (n_q, CHUNK), 1)
        ok = ((r % N_HEADS) == (c % N_HEADS)) & (c // N_HEADS <= r // N_HEADS) & (c < n_q)
        update([jnp.where(ok, s, NEG)], [vn])
        o_ref[0] = acc_ref[...] / l_ref[:, 0:1]


def _fox_sample(page_table, q, kn, vn, cn, cache_k, cache_v, c_page, tot_page, *, layer, n_pool, ppg, precise):
    n_seq, n_q, _ = q.shape
    n_pages = page_table.shape[1]
    n_keys = CHUNK * N_HEADS
    base = layer * n_pool
    pt = page_table.reshape(-1)

    def page(pg, ndim):
        return lambda n, g, pt_ref: (base + pt_ref[n * n_pages + g * ppg + pg],) + (0,) * (ndim - 1)

    seq3 = lambda n, g, pt_ref: (n, 0, 0)
    in_specs = [pl.BlockSpec((1, n_q, D_HEAD), seq3)] * 3 + [pl.BlockSpec((1, 1, CHUNK), seq3)]
    in_specs += [pl.BlockSpec((n_keys, D_HEAD), page(pg, 2)) for _ in range(2) for pg in range(ppg)]
    in_specs += [pl.BlockSpec((1, 1, n_keys), page(pg, 3)) for _ in range(2) for pg in range(ppg)]
    grid_spec = pltpu.PrefetchScalarGridSpec(
        num_scalar_prefetch=1,
        grid=(n_seq, n_pages // ppg),
        in_specs=in_specs,
        out_specs=pl.BlockSpec((1, n_q, D_HEAD), seq3),
        scratch_shapes=[pltpu.VMEM((n_q, LANES), F32), pltpu.VMEM((n_q, LANES), F32),
                        pltpu.VMEM((n_q, D_HEAD), F32), pltpu.VMEM((1, n_keys), F32)],
    )
    return pl.pallas_call(
        functools.partial(_fox_sample_kernel, ppg=ppg, n_q=n_q, precise=precise),
        grid_spec=grid_spec,
        out_shape=jax.ShapeDtypeStruct((n_seq, n_q, D_HEAD), F32),
        compiler_params=_params(2),
        name="fox_sample_split" if precise else "fox_sample",
    )(pt, q, kn, vn, cn, *([cache_k] * ppg), *([cache_v] * ppg), *([c_page] * ppg), *([tot_page] * ppg))


def _merge_kernel(oa_ref, ob_ref, oc_ref, gates_ref, x_ref, wb_ref, wo_ref, nf_ref, wr_ref, br_ref,
                  x1_ref, xn_ref, gate_ref, *, precise):
    mm = functools.partial(_dot, precise=precise)
    tm = x_ref.shape[0]
    merged = None
    for b, o_ref in enumerate((oa_ref, ob_ref, oc_ref)):
        term = mm(o_ref[...], wb_ref[b]) * _sigmoid(gates_ref[:, b * D_MODEL:(b + 1) * D_MODEL].astype(F32))
        merged = term if merged is None else merged + term
    x1 = x_ref[...] + mm(merged, wo_ref[...])
    x1_ref[...] = x1
    xn = _split2(_rms(x1, nf_ref[...]))
    xn_ref[...] = xn[0]
    logits = _dot(xn, wr_ref[...], precise=True) + br_ref[...]

    lane = lax.broadcasted_iota(jnp.int32, (tm, LANES), 1).astype(F32)
    big = float(LANES)
    is_grp = lane < float(N_GROUPS)
    mx = jnp.max(jnp.where(is_grp, logits, NEG), axis=-1, keepdims=True)
    p_grp = 1.0 / jnp.sum(jnp.where(is_grp, jnp.exp(jnp.where(is_grp, logits - mx, 0.0)), 0.0),
                          axis=-1, keepdims=True)
    g_star = jnp.min(jnp.where(is_grp & (logits == mx), lane, big), axis=-1, keepdims=True)
    base = float(LANE_EXPERT) + float(EXP_PER_GROUP) * g_star
    is_exp = (lane >= base) & (lane < base + float(EXP_PER_GROUP))
    le = jnp.where(is_exp, logits, NEG)
    t1 = jnp.max(le, axis=-1, keepdims=True)
    i1 = jnp.min(jnp.where(is_exp & (le == t1), lane, big), axis=-1, keepdims=True)
    le2 = jnp.where(lane == i1, NEG, le)
    t2 = jnp.max(le2, axis=-1, keepdims=True)
    i2 = jnp.min(jnp.where(is_exp & (lane != i1) & (le2 == t2), lane, big), axis=-1, keepdims=True)
    e2 = jnp.exp(t2 - t1)
    w1 = p_grp / (1.0 + e2)
    w2 = w1 * e2
    gate_ref[...] = jnp.where(lane == i1, w1, jnp.where(lane == i2, w2, 0.0))


def _merge(oa, ob, oc, p, x, wb, wo, nf, wr, br, tm, precise):
    r = x.shape[0]
    rows = lambda i: (i, 0)
    const2 = lambda i: (0, 0)
    return pl.pallas_call(
        functools.partial(_merge_kernel, precise=precise),
        grid=(r // tm,),
        in_specs=[
            pl.BlockSpec((tm, BRANCH_W), rows),
            pl.BlockSpec((tm, BRANCH_W), rows),
            pl.BlockSpec((tm, BRANCH_W), rows),
            pl.BlockSpec((tm, N_BRANCH * D_MODEL), rows),
            pl.BlockSpec((tm, D_MODEL), rows),
            pl.BlockSpec((N_BRANCH, BRANCH_W, D_MODEL), lambda i: (0, 0, 0)),
            pl.BlockSpec((D_MODEL, D_MODEL), const2),
            pl.BlockSpec((1, D_MODEL), const2),
            pl.BlockSpec((D_MODEL, LANES), const2),
            pl.BlockSpec((1, LANES), const2),
        ],
        out_specs=[
            pl.BlockSpec((tm, D_MODEL), rows),
            pl.BlockSpec((tm, D_MODEL), rows),
            pl.BlockSpec((tm, LANES), rows),
        ],
        out_shape=[jax.ShapeDtypeStruct((r, D_MODEL), F32), jax.ShapeDtypeStruct((r, D_MODEL), BF16),
                   jax.ShapeDtypeStruct((r, LANES), F32)],
        compiler_params=_params(1),
        name="merge_split" if precise else "merge",
    )(oa, ob, oc, p, x, wb, wo, nf, wr, br)


def _moe_kernel(xn_ref, gate_ref, x1_ref, w1_ref, w3_ref, w2_ref, nfin_ref, *out_refs, final):
    x2_ref = out_refs[0]
    e = pl.program_id(1)

    @pl.when(e == 0)
    def _():
        x2_ref[...] = x1_ref[...]

    xn = xn_ref[...]
    lane = lax.broadcasted_iota(jnp.int32, gate_ref.shape, 1)
    g_e = jnp.sum(jnp.where(lane == LANE_EXPERT + e, gate_ref[...], 0.0), axis=-1, keepdims=True)
    h = _silu(_dot(xn, w1_ref[0])) * _dot(xn, w3_ref[0])
    x2_ref[...] += _dot(h * g_e, w2_ref[0])

    if final:
        @pl.when(e == pl.num_programs(1) - 1)
        def _():
            out_refs[1][...] = _rms(x2_ref[...], nfin_ref[...])


def _moe(xn, gate, x1, w1, w3, w2, nfin, tm, final):
    r = x1.shape[0]
    rows = lambda i, e: (i, 0)
    out_specs = [pl.BlockSpec((tm, D_MODEL), rows)]
    out_shape = [jax.ShapeDtypeStruct((r, D_MODEL), F32)]
    if final:
        out_specs.append(pl.BlockSpec((tm, D_MODEL), rows))
        out_shape.append(jax.ShapeDtypeStruct((r, D_MODEL), F32))
    return pl.pallas_call(
        functools.partial(_moe_kernel, final=final),
        grid=(r // tm, N_EXPERTS),
        in_specs=[
            pl.BlockSpec((tm, D_MODEL), rows),
            pl.BlockSpec((tm, LANES), rows),
            pl.BlockSpec((tm, D_MODEL), rows),
            pl.BlockSpec((1, D_MODEL, D_FF), lambda i, e: (e, 0, 0)),
            pl.BlockSpec((1, D_MODEL, D_FF), lambda i, e: (e, 0, 0)),
            pl.BlockSpec((1, D_FF, D_MODEL), lambda i, e: (e, 0, 0)),
            pl.BlockSpec((1, D_MODEL), lambda i, e: (0, 0)),
        ],
        out_specs=out_specs,
        out_shape=out_shape,
        compiler_params=_params(2),
        name="moe",
    )(xn, gate, x1, w1, w3, w2, nfin)


def _rope_tables(pos):
    half = D_HEAD // 2
    inv = jnp.asarray((ROPE_BASE ** (-np.arange(half, dtype=np.float64) / half)).astype(np.float32))
    ang = pos.astype(F32)[:, None] * inv[None, :]
    cos, sin = jnp.cos(ang), jnp.sin(ang)
    return jnp.concatenate([cos, cos], axis=-1), jnp.concatenate([-sin, sin], axis=-1)


def _retention_tables(t_eff):
    idx = jnp.arange(CHUNK, dtype=F32)
    log_gamma = jnp.log(1.0 - jnp.exp2(-5.0 - jnp.arange(N_HEADS, dtype=F32)))
    incl = idx[:, None] >= idx[None, :]
    rel = jnp.where(incl, idx[:, None] - idx[None, :], 0.0)
    dmat = jnp.where(incl, jnp.exp(rel[None] * log_gamma[:, None, None]), 0.0)
    cross = jnp.exp((idx + 1.0)[:, None] * log_gamma[None, :])
    kdec = jnp.exp((t_eff - 1.0 - idx)[:, None] * log_gamma[None, :])
    tot = jnp.broadcast_to(jnp.exp(t_eff * log_gamma)[None, :], (CHUNK, N_HEADS))
    rtab = jnp.concatenate([cross, kdec, tot, jnp.zeros((CHUNK, LANES - 3 * N_HEADS), F32)], axis=1)
    return dmat, rtab


def _lanes_row(pairs):
    row = jnp.zeros((LANES,), F32)
    for start, vals in pairs:
        row = row.at[start:start + vals.shape[0]].set(vals.astype(F32))
    return row


def _pick(n, candidates):
    for c in candidates:
        if n % c == 0:
            return c
    raise ValueError(f"no tile in {candidates} divides {n}")


def kernel(x_prompt, x_sample, cache_k, cache_v, cache_logf, page_table, state_delta, state_conv, state_ret,
           meta_tokens, norm_mix, norm_ffn, norm_final, w_in, conv_w, a_log, dt_bias, a_norm, b_fbias, c_norm,
           w_branch, w_out, w_router_group, b_router_group, w_router_expert, b_router_expert, w1, w3, w2):
    nb, seq, _ = x_prompt.shape
    ns, ts, _ = x_sample.shape
    depth, n_pool = cache_k.shape[:2]
    n_pages = page_table.shape[1]
    assert seq % CHUNK == 0 and ts <= SAMPLE_ROWS and cache_k.shape[2] == CHUNK
    lp = CHUNK + seq
    n_chunks = lp // CHUNK
    rp = nb * lp
    rs = ns * SAMPLE_ROWS
    tm_p = _pick(rp, (512, 256, 128))
    tm_moe = _pick(rp, (768, 512, 256, 128))
    tm_s = _pick(rs, (256, 128, 64, 32, 16, 8))

    xp = jnp.concatenate([jnp.zeros((nb, PAD_FRONT, D_MODEL), F32),
                          jnp.broadcast_to(meta_tokens[None], (nb, N_META, D_MODEL)), x_prompt],
                         axis=1).reshape(rp, D_MODEL)
    xs = jnp.concatenate([x_sample, jnp.zeros((ns, SAMPLE_ROWS - ts, D_MODEL), F32)], axis=1).reshape(rs, D_MODEL)

    cos_p, sin_p = _rope_tables(jnp.arange(lp) - PAD_FRONT)
    cos_s, sin_s = _rope_tables(n_pages * CHUNK + jnp.arange(CHUNK))
    dmat_p, rtab_p = _retention_tables(float(CHUNK))
    dmat_s, rtab_s = _retention_tables(float(ts))

    assert (ts * N_HEADS) % 8 == 0
    n_cache = depth * n_pool
    cache_k2 = cache_k.reshape(n_cache * CHUNK * N_HEADS, D_HEAD)
    cache_v2 = cache_v.reshape(n_cache * CHUNK * N_HEADS, D_HEAD)
    c_page, tot_page = _page_cumsum(cache_logf.reshape(n_cache, CHUNK * N_HEADS))
    c_page = c_page.reshape(n_cache, 1, CHUNK * N_HEADS)
    tot_page = tot_page.reshape(n_cache, 1, CHUNK * N_HEADS)
    tb = _pick(lp, (384, 256, 128))
    par_p = _pick(nb, (PAR_PROMPT, 1))
    par_s = _pick(ns, (PAR_SAMPLE, 2, 1))
    ppg = _pick(n_pages, (8, 4, 2, 1))

    def sample_heads(a):
        return a.reshape(ns, SAMPLE_ROWS, N_HEADS, D_HEAD)[:, :ts].reshape(ns, ts * N_HEADS, D_HEAD)

    def prompt_rows(a):
        return a.reshape(nb, lp, a.shape[-1])[:, PAD_FRONT:]

    def sample_rows(a):
        return a.reshape(ns, SAMPLE_ROWS, a.shape[-1])[:, :ts]

    def col(a, c, w=BRANCH_W):
        return a[:, c * BRANCH_W:c * BRANCH_W + w]

    outs_p = [[] for _ in range(6)]
    outs_s = [[] for _ in range(6)]
    yp = ys = None
    for l in range(depth):
        final = l == depth - 1
        hp = not final
        wdt = F32 if hp else BF16
        w_main, w_small = _reorder_w_in(w_in, l, wdt)
        g_mix = norm_mix[l][None]
        p_p, ps_p = _proj(xp, g_mix, w_main, w_small, tm_p, 512, hp, F32 if hp else BF16)
        p_s, ps_s = _proj(xs, g_mix, w_main, w_small, tm_s, 512, hp)

        avec = jnp.zeros((8, LANES), F32)
        avec = avec.at[0].set(_lanes_row([(LANE_ALPHA, a_log[l])]))
        avec = avec.at[1].set(_lanes_row([(LANE_ALPHA, dt_bias[l])]))
        avec = avec.at[2].set(_lanes_row([(LANE_F, b_fbias[l])]))
        shared = (conv_w[l], avec, a_norm[l][None], c_norm[l][None])
        oa_p, oc_p, lf_p, cc_p, sd_p, sr_p = _seq(
            p_p, ps_p, cos_p, sin_p, *shared, dmat_p, rtab_p, n_seq=nb, n_chunks=n_chunks, tc_in=CHUNK,
            valid_hi=CHUNK, front_pad=PAD_FRONT, precise=hp, n_par=par_p)
        cb0 = jnp.pad(state_conv[l], ((0, 0), (8 - (CONV_W - 1), 0), (0, 0)))
        oa_s, oc_s, lf_s, cc_s, sd_s, sr_s = _seq(
            p_s, ps_s, cos_s, sin_s, *shared, dmat_s, rtab_s, n_seq=ns, n_chunks=1, tc_in=SAMPLE_ROWS,
            valid_hi=ts, front_pad=0, precise=hp, n_par=par_s, init=(state_delta[l], state_ret[l], cb0))

        ct = jnp.pad(jnp.swapaxes(cc_p[:, LANE_F:LANE_F + N_HEADS].reshape(nb, lp, N_HEADS), 1, 2),
                     ((0, 0), (0, 8 - N_HEADS), (0, 0)))
        ob_p = _fox_prompt(p_p, ct, nb, lp, tb, hp)
        cn = cc_s[:, LANE_F:LANE_F + N_HEADS].reshape(ns, SAMPLE_ROWS, N_HEADS)[:, :ts].reshape(ns, 1, ts * N_HEADS)
        cn = jnp.pad(cn, ((0, 0), (0, 0), (0, CHUNK - ts * N_HEADS)))
        o_s = _fox_sample(page_table, sample_heads(col(p_s, COL_BQ)), sample_heads(col(p_s, COL_BK)),
                          sample_heads(col(p_s, COL_BV)), cn, cache_k2, cache_v2, c_page, tot_page,
                          layer=l, n_pool=n_pool, ppg=ppg, precise=hp)
        ob_s = jnp.pad(o_s.reshape(ns, ts, BRANCH_W), ((0, 0), (0, SAMPLE_ROWS - ts), (0, 0))).reshape(rs, BRANCH_W)

        wr = jnp.concatenate([w_router_group[l], w_router_expert[l],
                              jnp.zeros((D_MODEL, LANES - N_GROUPS - N_EXPERTS), F32)], axis=1)
        br = _lanes_row([(LANE_GROUP, b_router_group[l]), (LANE_EXPERT, b_router_expert[l].reshape(-1))])[None]
        nf = norm_ffn[l][None]
        wb, wo = w_branch[l].astype(wdt), w_out[l].astype(wdt)
        x1_p, xn_p, gate_p = _merge(oa_p, ob_p, oc_p, p_p, xp, wb, wo, nf, wr, br, min(tm_p, 256), hp)
        x1_s, xn_s, gate_s = _merge(oa_s, ob_s, oc_s, p_s, xs, wb, wo, nf, wr, br, tm_s, hp)

        ew1 = w1[l].reshape(N_EXPERTS, D_MODEL, D_FF).astype(BF16)
        ew3 = w3[l].reshape(N_EXPERTS, D_MODEL, D_FF).astype(BF16)
        ew2 = w2[l].reshape(N_EXPERTS, D_FF, D_MODEL).astype(BF16)
        res_p = _moe(xn_p, gate_p, x1_p, ew1, ew3, ew2, norm_final[None], tm_moe, final)
        res_s = _moe(xn_s, gate_s, x1_s, ew1, ew3, ew2, norm_final[None], tm_s, final)
        xp, xs = res_p[0], res_s[0]
        if final:
            yp, ys = res_p[1], res_s[1]

        for lst, rows_of, p, lf_a, sd, sr in ((outs_p, prompt_rows, p_p, lf_p, sd_p, sr_p),
                                              (outs_s, sample_rows, p_s, lf_s, sd_s, sr_s)):
            k_new = rows_of(col(p, COL_BK)).astype(F32)
            v_new = rows_of(col(p, COL_BV)).astype(F32)
            lst[0].append(k_new.reshape(k_new.shape[:2] + (N_HEADS, D_HEAD)))
            lst[1].append(v_new.reshape(v_new.shape[:2] + (N_HEADS, D_HEAD)))
            lst[2].append(rows_of(lf_a[:, LANE_F:LANE_F + N_HEADS]))
            lst[3].append(sd)
            lst[4].append(rows_of(col(p, COL_AQKV, A_QKV))[:, -(CONV_W - 1):].astype(F32))
            lst[5].append(sr)

    y_prompt = yp.reshape(nb, lp, D_MODEL)[:, CHUNK:]
    y_sample = ys.reshape(ns, SAMPLE_ROWS, D_MODEL)[:, :ts]
    return (y_prompt, y_sample, *(jnp.stack(a) for a in outs_p), *(jnp.stack(a) for a in outs_s))
```

```python
import functools

import jax
import jax.numpy as jnp
import numpy as np
from jax import lax
from jax.experimental import pallas as pl
from jax.experimental.pallas import tpu as pltpu

F32 = jnp.float32
BF16 = jnp.bfloat16

D_MODEL = 1024
N_META = 16
CHUNK = 128
PAD_FRONT = CHUNK - N_META
N_HEADS = 4
D_HEAD = 128
BRANCH_W = N_HEADS * D_HEAD
A_QKV = 3 * BRANCH_W
CONV_W = 4
N_BRANCH = 3
N_GROUPS = 4
EXP_PER_GROUP = 4
N_EXPERTS = N_GROUPS * EXP_PER_GROUP
D_FF = 512
ROPE_BASE = 10000.0
EPS = 1e-6
NEG = -1e30
SAMPLE_ROWS = 8
LANES = 128

COL_GATES = 0
COL_AQKV = 6
COL_AZ = 9
COL_BQ, COL_BK, COL_BV = 10, 11, 12
COL_CQ, COL_CK, COL_CV, COL_CG = 13, 14, 15, 16
N_MAIN = 17 * BRANCH_W
LANE_BETA, LANE_ALPHA, LANE_F = 0, 4, 8
LANE_GROUP, LANE_EXPERT = 0, 4

VMEM_LIMIT = 56 * 1024 * 1024
PAR_PROMPT, PAR_SAMPLE = 4, 4

NN = (((1,), (0,)), ((), ()))
NT = (((1,), (1,)), ((), ()))
TN = (((0,), (0,)), ((), ()))


def _split2(x):
    hi = x.astype(BF16)
    return hi, (x - hi.astype(F32)).astype(BF16)


def _dot(a, b, dims=NN, precise=False):
    def dg(x, y):
        return lax.dot_general(x, y, dims, preferred_element_type=F32)

    if not precise:
        a = a[0] if isinstance(a, tuple) else a.astype(BF16)
        b = b[0] if isinstance(b, tuple) else b.astype(BF16)
        return dg(a, b)
    a_hi, a_lo = a if isinstance(a, tuple) else _split2(a)
    b_hi, b_lo = b if isinstance(b, tuple) else _split2(b)
    return dg(a_hi, b_hi) + dg(a_hi, b_lo) + dg(a_lo, b_hi)


def _split3(x):
    x1 = x.astype(BF16)
    r1 = x - x1.astype(F32)
    x2 = r1.astype(BF16)
    x3 = (r1 - x2.astype(F32)).astype(BF16)
    return x1, x2, x3


def _sigmoid(x):
    return 1.0 / (1.0 + jnp.exp(-x))


def _silu(x):
    return x * _sigmoid(x)


def _softplus(x):
    return jnp.maximum(x, 0.0) + jnp.log1p(jnp.exp(-jnp.abs(x)))


def _rms(x, g):
    return x * lax.rsqrt(jnp.mean(x * x, axis=-1, keepdims=True) + EPS) * g


def _params(n_axes):
    return pltpu.CompilerParams(dimension_semantics=("arbitrary",) * n_axes, vmem_limit_bytes=VMEM_LIMIT)


SEGMENTS = ((0, A_QKV + 8 * BRANCH_W + 12, N_BRANCH * D_MODEL),
            (N_BRANCH * D_MODEL, 0, A_QKV + BRANCH_W),
            (N_BRANCH * D_MODEL + 4 * BRANCH_W, A_QKV + BRANCH_W + 8, 3 * BRANCH_W),
            (N_BRANCH * D_MODEL + 7 * BRANCH_W, A_QKV + 4 * BRANCH_W + 12, 4 * BRANCH_W))
OLD_SMALL_A = A_QKV + BRANCH_W
OLD_SMALL_F = A_QKV + 4 * BRANCH_W + 8


def _reorder_kernel(blk_ref, sh_ref, a_ref, b_ref, o_ref, *, shifts):
    j = pl.program_id(0)
    for shift in shifts:
        @pl.when(sh_ref[j] == shift)
        def _(shift=shift):
            a = a_ref[...]
            if shift:
                a = jnp.concatenate([a[:, shift:], b_ref[:, :shift]], axis=1)
            o_ref[...] = a.astype(o_ref.dtype)


def _small_kernel(a_ref, f_ref, o_ref):
    lane = lax.broadcasted_iota(jnp.int32, a_ref.shape, 1)
    o_ref[...] = jnp.where(lane < LANE_F, a_ref[...], jnp.where(lane < LANE_F + N_HEADS, f_ref[...], 0.0)
                           ).astype(o_ref.dtype)


def _reorder_w_in(w_in, layer, dtype):
    n_old = w_in.shape[2]
    last = (n_old - 1) // LANES
    blks, shifts = [], []
    for new0, old0, width in SEGMENTS:
        assert new0 % LANES == 0 and width % LANES == 0
        for t in range(width // LANES):
            blks.append(old0 // LANES + t)
            shifts.append(old0 % LANES)
    grid_spec = pltpu.PrefetchScalarGridSpec(
        num_scalar_prefetch=2,
        grid=(len(blks),),
        in_specs=[pl.BlockSpec((None, D_MODEL, LANES), lambda j, blk, sh: (layer, 0, blk[j])),
                  pl.BlockSpec((None, D_MODEL, LANES), lambda j, blk, sh: (layer, 0, jnp.minimum(blk[j] + 1, last)))],
        out_specs=pl.BlockSpec((D_MODEL, LANES), lambda j, blk, sh: (0, j)),
    )
    w_main = pl.pallas_call(
        functools.partial(_reorder_kernel, shifts=tuple(sorted(set(shifts)))),
        grid_spec=grid_spec,
        out_shape=jax.ShapeDtypeStruct((D_MODEL, N_MAIN), dtype),
        compiler_params=_params(1),
        name="reorder_w_in",
    )(jnp.asarray(blks, jnp.int32), jnp.asarray(shifts, jnp.int32), w_in, w_in)
    assert OLD_SMALL_A % LANES == LANE_BETA and OLD_SMALL_F % LANES == LANE_F
    w_small = pl.pallas_call(
        _small_kernel,
        grid=(1,),
        in_specs=[pl.BlockSpec((None, D_MODEL, LANES), lambda i: (layer, 0, OLD_SMALL_A // LANES)),
                  pl.BlockSpec((None, D_MODEL, LANES), lambda i: (layer, 0, OLD_SMALL_F // LANES))],
        out_specs=pl.BlockSpec((D_MODEL, LANES), lambda i: (0, 0)),
        out_shape=jax.ShapeDtypeStruct((D_MODEL, LANES), dtype),
        compiler_params=_params(1),
        name="reorder_w_small",
    )(w_in, w_in)
    return w_main, w_small


def _proj_kernel(x_ref, g_ref, w_ref, ws_ref, p_ref, ps_ref, xn_ref, *, tn, precise):
    j = pl.program_id(1)

    @pl.when(j == 0)
    def _():
        xn = _split2(_rms(x_ref[...], g_ref[...]))
        xn_ref[0] = xn[0]
        if precise:
            xn_ref[1] = xn[1]
        ps_ref[...] = _dot(xn, ws_ref[...], precise=precise)

    if precise:
        p_ref[...] = _dot((xn_ref[0], xn_ref[1]), w_ref[...], precise=True).astype(p_ref.dtype)
    else:
        p_ref[...] = _dot(xn_ref[0], w_ref[:, pl.ds(pl.multiple_of(j * tn, tn), tn)]).astype(p_ref.dtype)


def _proj(x, g, w_main, w_small, tm, tn, precise, out_dtype=F32):
    r = x.shape[0]
    w_spec = (pl.BlockSpec((D_MODEL, tn), lambda i, j: (0, j)) if precise else
              pl.BlockSpec((D_MODEL, N_MAIN), lambda i, j: (0, 0)))
    return pl.pallas_call(
        functools.partial(_proj_kernel, tn=tn, precise=precise),
        grid=(r // tm, N_MAIN // tn),
        in_specs=[pl.BlockSpec((tm, D_MODEL), lambda i, j: (i, 0)),
                  pl.BlockSpec((1, D_MODEL), lambda i, j: (0, 0)),
                  w_spec,
                  pl.BlockSpec((D_MODEL, LANES), lambda i, j: (0, 0))],
        out_specs=[pl.BlockSpec((tm, tn), lambda i, j: (i, j)),
                   pl.BlockSpec((tm, LANES), lambda i, j: (i, 0))],
        out_shape=[jax.ShapeDtypeStruct((r, N_MAIN), out_dtype), jax.ShapeDtypeStruct((r, LANES), F32)],
        scratch_shapes=[pltpu.VMEM((2 if precise else 1, tm, D_MODEL), BF16)],
        compiler_params=_params(2),
        name="proj_split" if precise else "proj",
    )(x, g, w_main, w_small)


def _delta_heads(qs, ks, vs, betas, gcols, grows, states, incl, strict, precise):
    mm = functools.partial(_dot, precise=precise)
    n = range(len(qs))
    egs = [jnp.exp(gcols[i]) for i in n]
    decays = [jnp.where(incl, jnp.exp(jnp.where(incl, gcols[i] - grows[i], 0.0)), 0.0) for i in n]
    kbs = [ks[i] * betas[i] for i in n]
    ms = [jnp.where(strict, mm(kbs[i], ks[i], NT) * decays[i], 0.0) for i in n]
    rs = [-m for m in ms]
    pows = [_dot(r, r) for r in rs]
    for it in range(6):
        rps = [_dot(rs[i], pows[i]) for i in n]
        if it < 5:
            nxt = [_dot(pows[i], pows[i]) for i in n]
        rs = [rs[i] + pows[i] + rps[i] for i in n]
        if it < 5:
            pows = nxt
    for _ in range(2 if precise else 1):
        es = [-(ms[i] + rs[i] + _dot(ms[i], rs[i], precise=True)) for i in n]
        res = [_dot(rs[i], es[i]) for i in n]
        rs = [rs[i] + es[i] + res[i] for i in n]
    vbs = [vs[i] * betas[i] for i in n]
    kbes = [kbs[i] * egs[i] for i in n]
    us = [vbs[i] + mm(rs[i], vbs[i]) for i in n]
    ws = [kbes[i] + mm(rs[i], kbes[i]) for i in n]
    attns = [mm(qs[i], ks[i], NT) * decays[i] for i in n]
    v_news = [us[i] - mm(ws[i], states[i]) for i in n]
    outs = [mm(qs[i] * egs[i], states[i]) + mm(attns[i], v_news[i]) for i in n]
    g_lasts = [g[CHUNK - 1:CHUNK, :] for g in gcols]
    new_states = [states[i] * jnp.exp(g_lasts[i]) + mm(ks[i] * jnp.exp(g_lasts[i] - gcols[i]), v_news[i], TN)
                  for i in n]
    return outs, new_states


def _seq_kernel(*refs, n_par, tc_in, valid_hi, front_pad, has_init, precise):
    if has_init:
        (aqkv_ref, az_ref, cq_ref, ck_ref, cv_ref, cg_ref, sm_ref, cos_ref, sin_ref,
         convw_ref, avec_ref, anorm_ref, cnorm_ref, dmat_ref, rtab_ref,
         sd0_ref, sr0_ref, cb0_ref,
         oa_ref, oc_ref, lf_ref, cc_ref, sd_ref, sr_ref, xbuf, cbuf, carry) = refs
    else:
        (aqkv_ref, az_ref, cq_ref, ck_ref, cv_ref, cg_ref, sm_ref, cos_ref, sin_ref,
         convw_ref, avec_ref, anorm_ref, cnorm_ref, dmat_ref, rtab_ref,
         oa_ref, oc_ref, lf_ref, cc_ref, sd_ref, sr_ref, xbuf, cbuf, carry) = refs
    mm = functools.partial(_dot, precise=precise)
    c = pl.program_id(1)

    def ext(ref, s, cols=None):
        v = (ref[s] if cols is None else ref[s, :, cols]).astype(F32)
        if tc_in == CHUNK:
            return v
        return jnp.concatenate([v, jnp.zeros((CHUNK - tc_in, v.shape[1]), v.dtype)], axis=0)

    @pl.when(c == 0)
    def _():
        carry[...] = jnp.zeros_like(carry)
        if has_init:
            xbuf[:, 0:8, :] = cb0_ref[...]
            sd_ref[...] = sd0_ref[...]
            sr_ref[...] = sr0_ref[...]
        else:
            xbuf[:, 0:8, :] = jnp.zeros((n_par, 8, A_QKV), F32)
            sd_ref[...] = jnp.zeros_like(sd_ref)
            sr_ref[...] = jnp.zeros_like(sr_ref)

    rows = lax.broadcasted_iota(jnp.int32, (CHUNK, LANES), 0)
    lane = lax.broadcasted_iota(jnp.int32, (CHUNK, LANES), 1)
    lo = jnp.where(c == 0, front_pad, 0)
    valid = (rows >= lo) & (rows < valid_hi)
    incl = rows >= lane
    strict = rows > lane
    is_g = (lane >= LANE_ALPHA) & (lane < LANE_ALPHA + N_HEADS)
    is_f = (lane >= LANE_F) & (lane < LANE_F + N_HEADS)
    tril = jnp.where(incl, 1.0, 0.0).astype(BF16)
    av = avec_ref[...]
    cos2 = cos_ref[...]
    sin2 = sin_ref[...]
    rtab = rtab_ref[...]

    heads = range(N_HEADS)
    hsl = [slice(h * D_HEAD, (h + 1) * D_HEAD) for h in heads]
    chains = [(s, h) for s in range(n_par) for h in heads]
    qs, ks, vs, betas, gcols, grows = [], [], [], [], [], []
    for s in range(n_par):
        for j in range(A_QKV // LANES):
            cols = slice(j * LANES, (j + 1) * LANES)
            xbuf[s, 8:8 + CHUNK, cols] = ext(aqkv_ref, s, cols)
            acc = xbuf[s, 5:5 + CHUNK, cols] * convw_ref[0:1, cols]
            for i in range(1, CONV_W):
                acc = acc + xbuf[s, 5 + i:5 + i + CHUNK, cols] * convw_ref[i:i + 1, cols]
            cbuf[s, :, cols] = jnp.where(valid, _silu(acc), 0.0)
            xbuf[s, 0:8, cols] = xbuf[s, CHUNK:CHUNK + 8, cols]

        sm = ext(sm_ref, s)
        g_all = -jnp.exp(av[0:1, :]) * _softplus(sm + av[1:2, :])
        lf_all = -_softplus(-(sm + av[2:3, :]))
        z = jnp.where(valid & is_g, g_all, jnp.where(valid & is_f, lf_all, 0.0))
        cum = sum(jnp.dot(tril, part, preferred_element_type=F32) for part in _split3(z))
        cum_t = cum.T
        cc = cum + carry[s]
        carry[s] = cc[CHUNK - 1:CHUNK, :]
        lf_ref[s] = z[:tc_in]
        cc_ref[s] = cc[:tc_in]

        beta_all = _sigmoid(sm)
        for h in heads:
            q = cbuf[s, :, hsl[h]]
            k = cbuf[s, :, BRANCH_W + h * D_HEAD:BRANCH_W + (h + 1) * D_HEAD]
            qs.append(q * lax.rsqrt(jnp.sum(q * q, axis=-1, keepdims=True) + EPS) * (D_HEAD ** -0.5))
            ks.append(k * lax.rsqrt(jnp.sum(k * k, axis=-1, keepdims=True) + EPS))
            vs.append(cbuf[s, :, 2 * BRANCH_W + h * D_HEAD:2 * BRANCH_W + (h + 1) * D_HEAD])
            betas.append(beta_all[:, LANE_BETA + h:LANE_BETA + h + 1])
            gcols.append(cum[:, LANE_ALPHA + h:LANE_ALPHA + h + 1])
            grows.append(cum_t[LANE_ALPHA + h:LANE_ALPHA + h + 1, :])

    outs, new_states = _delta_heads(qs, ks, vs, betas, gcols, grows, [sd_ref[s, h] for s, h in chains],
                                    incl, strict, precise)
    for i, (s, h) in enumerate(chains):
        sd_ref[s, h] = new_states[i]
        oa = _rms(outs[i], anorm_ref[...]) * _silu(ext(az_ref, s, hsl[h]))
        oa_ref[s, :, hsl[h]] = oa[:tc_in]

    qcs = [ext(cq_ref, s, hsl[h]) for s, h in chains]
    kcs = [ext(ck_ref, s, hsl[h]) for s, h in chains]
    vcs = [ext(cv_ref, s, hsl[h]) for s, h in chains]
    qcs = [q * cos2 + pltpu.roll(q, D_HEAD // 2, 1) * sin2 for q in qcs]
    kcs = [(k * cos2 + pltpu.roll(k, D_HEAD // 2, 1) * sin2) * (D_HEAD ** -0.5) for k in kcs]
    srs = [sr_ref[s, h] for s, h in chains]
    scores = [mm(qcs[i], kcs[i], NT) * dmat_ref[h] for i, (s, h) in enumerate(chains)]
    crosses = [mm(qcs[i] * rtab[:, h:h + 1], srs[i]) for i, (s, h) in enumerate(chains)]
    kvs = [mm(kcs[i] * rtab[:, 4 + h:5 + h], vcs[i], TN) for i, (s, h) in enumerate(chains)]
    inners = [mm(scores[i], vcs[i]) for i in range(len(chains))]
    for i, (s, h) in enumerate(chains):
        sr_ref[s, h] = srs[i] * rtab[0:1, 8 + h:9 + h] + kvs[i]
        oc = _rms(inners[i] + crosses[i], cnorm_ref[...]) * _silu(ext(cg_ref, s, hsl[h]))
        oc_ref[s, :, hsl[h]] = oc[:tc_in]


def _seq(p, ps, cos2, sin2, convw, avec, anorm, cnorm, dmat, rtab, *, n_seq, n_chunks, tc_in,
         valid_hi, front_pad, precise, n_par, init=None):
    r = p.shape[0]
    lp = n_chunks * tc_in
    p3 = p.reshape(n_seq, lp, N_MAIN)
    ps3 = ps.reshape(n_seq, lp, LANES)

    def rows(width, col):
        return pl.BlockSpec((n_par, tc_in, width), lambda n, c: (n, c, col))

    const2 = lambda n, c: (0, 0)
    state_spec = pl.BlockSpec((n_par, N_HEADS, D_HEAD, D_HEAD), lambda n, c: (n, 0, 0, 0))
    in_specs = [
        rows(A_QKV, COL_AQKV // 3),
        rows(BRANCH_W, COL_AZ),
        rows(BRANCH_W, COL_CQ),
        rows(BRANCH_W, COL_CK),
        rows(BRANCH_W, COL_CV),
        rows(BRANCH_W, COL_CG),
        rows(LANES, 0),
        pl.BlockSpec((CHUNK, D_HEAD), lambda n, c: (c, 0)),
        pl.BlockSpec((CHUNK, D_HEAD), lambda n, c: (c, 0)),
        pl.BlockSpec((CONV_W, A_QKV), const2),
        pl.BlockSpec((8, LANES), const2),
        pl.BlockSpec((1, D_HEAD), const2),
        pl.BlockSpec((1, D_HEAD), const2),
        pl.BlockSpec((N_HEADS, CHUNK, CHUNK), lambda n, c: (0, 0, 0)),
        pl.BlockSpec((CHUNK, LANES), const2),
    ]
    args = [p3, p3, p3, p3, p3, p3, ps3, cos2, sin2, convw, avec, anorm, cnorm, dmat, rtab]
    if init is not None:
        in_specs += [state_spec, state_spec, pl.BlockSpec((n_par, 8, A_QKV), lambda n, c: (n, 0, 0))]
        args += list(init)
    out_specs = [rows(BRANCH_W, 0), rows(BRANCH_W, 0), rows(LANES, 0), rows(LANES, 0), state_spec, state_spec]
    out_shape = [
        jax.ShapeDtypeStruct((n_seq, lp, BRANCH_W), F32),
        jax.ShapeDtypeStruct((n_seq, lp, BRANCH_W), F32),
        jax.ShapeDtypeStruct((n_seq, lp, LANES), F32),
        jax.ShapeDtypeStruct((n_seq, lp, LANES), F32),
        jax.ShapeDtypeStruct((n_seq, N_HEADS, D_HEAD, D_HEAD), F32),
        jax.ShapeDtypeStruct((n_seq, N_HEADS, D_HEAD, D_HEAD), F32),
    ]
    kern = functools.partial(_seq_kernel, n_par=n_par, tc_in=tc_in, valid_hi=valid_hi, front_pad=front_pad,
                             has_init=init is not None, precise=precise)
    oa, oc, lf, cc, sd, sr = pl.pallas_call(
        kern,
        grid=(n_seq // n_par, n_chunks),
        in_specs=in_specs,
        out_specs=out_specs,
        out_shape=out_shape,
        scratch_shapes=[pltpu.VMEM((n_par, CHUNK + 8, A_QKV), F32), pltpu.VMEM((n_par, CHUNK, A_QKV), F32),
                        pltpu.VMEM((n_par, 1, LANES), F32)],
        compiler_params=_params(2),
        name=("seq_prompt" if init is None else "seq_sample") + ("_split" if precise else ""),
    )(*args)
    return (oa.reshape(r, BRANCH_W), oc.reshape(r, BRANCH_W), lf.reshape(r, LANES), cc.reshape(r, LANES), sd, sr)


def _fox_prompt_kernel(q_ref, k_ref, v_ref, ct_ref, o_ref, m_ref, l_ref, acc_ref, *, tb, precise):
    mm = functools.partial(_dot, precise=precise)
    i = pl.program_id(1)
    j = pl.program_id(2)

    @pl.when(j == 0)
    def _():
        m_ref[...] = jnp.full_like(m_ref, NEG)
        l_ref[...] = jnp.zeros_like(l_ref)
        acc_ref[...] = jnp.zeros_like(acc_ref)

    def block(masked):
        if masked:
            qpos = i * tb + lax.broadcasted_iota(jnp.int32, (tb, tb), 0)
            kpos = j * tb + lax.broadcasted_iota(jnp.int32, (tb, tb), 1)
            ok = (kpos <= qpos) & (kpos >= PAD_FRONT)
        for h in range(N_HEADS):
            hs = slice(h * D_HEAD, (h + 1) * D_HEAD)
            s = mm(q_ref[:, hs].astype(F32) * (D_HEAD ** -0.5), k_ref[:, hs], NT) - ct_ref[0, h:h + 1, :]
            if masked:
                s = jnp.where(ok, s, NEG)
            m_old = m_ref[h][:, 0:1]
            m_new = jnp.maximum(m_old, jnp.max(s, axis=-1, keepdims=True))
            alpha = jnp.exp(m_old - m_new)
            p = jnp.exp(s - m_new)
            l_ref[h] = alpha * l_ref[h] + jnp.sum(p, axis=-1, keepdims=True)
            acc_ref[h] = alpha * acc_ref[h] + mm(p, v_ref[:, hs])
            m_ref[h] = jnp.broadcast_to(m_new, (tb, LANES))

    needs_mask = (j == i) | (j == 0)
    pl.when(needs_mask)(functools.partial(block, True))
    pl.when((j < i) & (j > 0))(functools.partial(block, False))

    @pl.when(j == i)
    def _():
        qvalid = (i * tb + lax.broadcasted_iota(jnp.int32, (tb, D_HEAD), 0)) >= PAD_FRONT
        for h in range(N_HEADS):
            o = acc_ref[h] / l_ref[h][:, 0:1]
            o_ref[:, h * D_HEAD:(h + 1) * D_HEAD] = jnp.where(qvalid, o, 0.0)


def _fox_prompt(p, ct, n_seq, lp, tb, precise):
    r = p.shape[0]
    nb = lp // tb
    return pl.pallas_call(
        functools.partial(_fox_prompt_kernel, tb=tb, precise=precise),
        grid=(n_seq, nb, nb),
        in_specs=[
            pl.BlockSpec((tb, BRANCH_W), lambda n, i, j: (n * nb + i, COL_BQ)),
            pl.BlockSpec((tb, BRANCH_W), lambda n, i, j: (n * nb + jnp.minimum(i, j), COL_BK)),
            pl.BlockSpec((tb, BRANCH_W), lambda n, i, j: (n * nb + jnp.minimum(i, j), COL_BV)),
            pl.BlockSpec((1, 8, tb), lambda n, i, j: (n, 0, jnp.minimum(i, j))),
        ],
        out_specs=pl.BlockSpec((tb, BRANCH_W), lambda n, i, j: (n * nb + i, 0)),
        out_shape=jax.ShapeDtypeStruct((r, BRANCH_W), F32),
        scratch_shapes=[pltpu.VMEM((N_HEADS, tb, LANES), F32), pltpu.VMEM((N_HEADS, tb, LANES), F32),
                        pltpu.VMEM((N_HEADS, tb, D_HEAD), F32)],
        compiler_params=_params(3),
        name="fox_prompt",
    )(p, p, p, ct)


def _page_cumsum_kernel(lf_ref, c_ref, tot_ref):
    n = CHUNK * N_HEADS
    ri = lax.broadcasted_iota(jnp.int32, (n, n), 0)
    ci = lax.broadcasted_iota(jnp.int32, (n, n), 1)
    same_head = (ri % N_HEADS) == (ci % N_HEADS)
    upper = jnp.where(same_head & (ri <= ci), 1.0, 0.0).astype(BF16)
    total = jnp.where(same_head, 1.0, 0.0).astype(BF16)
    parts = _split3(lf_ref[...])
    c_ref[...] = sum(jnp.dot(x, upper, preferred_element_type=F32) for x in parts)
    tot_ref[...] = sum(jnp.dot(x, total, preferred_element_type=F32) for x in parts)


def _page_cumsum(lf):
    n_pages, n = lf.shape
    tp = _pick(n_pages, (256, 128, 64, 32, 16, 8, n_pages))
    spec = pl.BlockSpec((tp, n), lambda i: (i, 0))
    return pl.pallas_call(
        _page_cumsum_kernel,
        grid=(n_pages // tp,),
        in_specs=[spec],
        out_specs=[spec, spec],
        out_shape=[jax.ShapeDtypeStruct((n_pages, n), F32)] * 2,
        compiler_params=_params(1),
        name="page_cumsum",
    )(lf)


def _fox_sample_kernel(*refs, ppg, n_q, precise):
    q_ref, kn_ref, vn_ref, cn_ref = refs[1:5]
    k_refs = refs[5:5 + ppg]
    v_refs = refs[5 + ppg:5 + 2 * ppg]
    c_refs = refs[5 + 2 * ppg:5 + 3 * ppg]
    tot_refs = refs[5 + 3 * ppg:5 + 4 * ppg]
    o_ref, m_ref, l_ref, acc_ref, carry = refs[5 + 4 * ppg:]
    mm = functools.partial(_dot, precise=precise)
    g = pl.program_id(1)
    n_keys = CHUNK * N_HEADS

    @pl.when(g == 0)
    def _():
        m_ref[...] = jnp.full_like(m_ref, NEG)
        l_ref[...] = jnp.zeros_like(l_ref)
        acc_ref[...] = jnp.zeros_like(acc_ref)
        carry[...] = jnp.zeros_like(carry)

    row = lax.broadcasted_iota(jnp.int32, (n_q, n_keys), 0)
    lane = lax.broadcasted_iota(jnp.int32, (n_q, n_keys), 1)
    same_head = (row % N_HEADS) == (lane % N_HEADS)
    scale = D_HEAD ** -0.5
    q = _split2(q_ref[0])

    def update(scores, values):
        top = scores[0]
        for s in scores[1:]:
            top = jnp.maximum(top, s)
        m_old = m_ref[:, 0:1]
        m_new = jnp.maximum(m_old, jnp.max(top, axis=-1, keepdims=True))
        alpha = jnp.exp(m_old - m_new)
        probs = [jnp.exp(s - m_new) for s in scores]
        mass = probs[0]
        for p in probs[1:]:
            mass = mass + p
        pv = mm(probs[0], values[0])
        for p, v in zip(probs[1:], values[1:]):
            pv = pv + mm(p, v)
        l_ref[...] = alpha * l_ref[...] + jnp.sum(mass, axis=-1, keepdims=True)
        acc_ref[...] = alpha * acc_ref[...] + pv
        m_ref[...] = jnp.broadcast_to(m_new, m_ref.shape)

    offset = carry[...]
    scores = []
    for pg in range(ppg):
        s = mm(q, k_refs[pg][...], NT) * scale - (offset + c_refs[pg][0])
        scores.append(jnp.where(same_head, s, NEG))
        offset = offset + tot_refs[pg][0]
    carry[...] = offset
    update(scores, [v_refs[pg][...] for pg in range(ppg)])

    @pl.when(g == pl.num_programs(1) - 1)
    def _():
        zpad = jnp.zeros((CHUNK - n_q, D_HEAD), F32)
        kn = jnp.concatenate([kn_ref[0], zpad], axis=0)
        vn = jnp.concatenate([vn_ref[0], zpad], axis=0)
        bias = carry[:, 0:CHUNK] + cn_ref[0]
        s = mm(q, kn, NT) * scale - bias
        r = lax.broadcasted_iota(jnp.int32, (n_q, CHUNK), 0)
        c = lax.broadcasted_iota(jnp.int32, (n_q, CHUNK), 1)
        ok = ((r % N_HEADS) == (c % N_HEADS)) & (c // N_HEADS <= r // N_HEADS) & (c < n_q)
        update([jnp.where(ok, s, NEG)], [vn])
        o_ref[0] = acc_ref[...] / l_ref[:, 0:1]


def _fox_sample(page_table, q, kn, vn, cn, cache_k, cache_v, c_page, tot_page, *, layer, n_pool, ppg, precise):
    n_seq, n_q, _ = q.shape
    n_pages = page_table.shape[1]
    n_keys = CHUNK * N_HEADS
    base = layer * n_pool
    pt = page_table.reshape(-1)

    def page(pg, ndim):
        return lambda n, g, pt_ref: (base + pt_ref[n * n_pages + g * ppg + pg],) + (0,) * (ndim - 1)

    seq3 = lambda n, g, pt_ref: (n, 0, 0)
    in_specs = [pl.BlockSpec((1, n_q, D_HEAD), seq3)] * 3 + [pl.BlockSpec((1, 1, CHUNK), seq3)]
    in_specs += [pl.BlockSpec((n_keys, D_HEAD), page(pg, 2)) for _ in range(2) for pg in range(ppg)]
    in_specs += [pl.BlockSpec((1, 1, n_keys), page(pg, 3)) for _ in range(2) for pg in range(ppg)]
    grid_spec = pltpu.PrefetchScalarGridSpec(
        num_scalar_prefetch=1,
        grid=(n_seq, n_pages // ppg),
        in_specs=in_specs,
        out_specs=pl.BlockSpec((1, n_q, D_HEAD), seq3),
        scratch_shapes=[pltpu.VMEM((n_q, LANES), F32), pltpu.VMEM((n_q, LANES), F32),
                        pltpu.VMEM((n_q, D_HEAD), F32), pltpu.VMEM((1, n_keys), F32)],
    )
    return pl.pallas_call(
        functools.partial(_fox_sample_kernel, ppg=ppg, n_q=n_q, precise=precise),
        grid_spec=grid_spec,
        out_shape=jax.ShapeDtypeStruct((n_seq, n_q, D_HEAD), F32),
        compiler_params=_params(2),
        name="fox_sample_split" if precise else "fox_sample",
    )(pt, q, kn, vn, cn, *([cache_k] * ppg), *([cache_v] * ppg), *([c_page] * ppg), *([tot_page] * ppg))


def _merge_kernel(oa_ref, ob_ref, oc_ref, gates_ref, x_ref, wb_ref, wo_ref, nf_ref, wr_ref, br_ref,
                  x1_ref, xn_ref, gate_ref, *, precise):
    mm = functools.partial(_dot, precise=precise)
    tm = x_ref.shape[0]
    merged = None
    for b, o_ref in enumerate((oa_ref, ob_ref, oc_ref)):
        term = mm(o_ref[...], wb_ref[b]) * _sigmoid(gates_ref[:, b * D_MODEL:(b + 1) * D_MODEL].astype(F32))
        merged = term if merged is None else merged + term
    x1 = x_ref[...] + mm(merged, wo_ref[...])
    x1_ref[...] = x1
    xn = _split2(_rms(x1, nf_ref[...]))
    xn_ref[...] = xn[0]
    logits = _dot(xn, wr_ref[...], precise=True) + br_ref[...]

    lane = lax.broadcasted_iota(jnp.int32, (tm, LANES), 1).astype(F32)
    big = float(LANES)
    is_grp = lane < float(N_GROUPS)
    mx = jnp.max(jnp.where(is_grp, logits, NEG), axis=-1, keepdims=True)
    p_grp = 1.0 / jnp.sum(jnp.where(is_grp, jnp.exp(jnp.where(is_grp, logits - mx, 0.0)), 0.0),
                          axis=-1, keepdims=True)
    g_star = jnp.min(jnp.where(is_grp & (logits == mx), lane, big), axis=-1, keepdims=True)
    base = float(LANE_EXPERT) + float(EXP_PER_GROUP) * g_star
    is_exp = (lane >= base) & (lane < base + float(EXP_PER_GROUP))
    le = jnp.where(is_exp, logits, NEG)
    t1 = jnp.max(le, axis=-1, keepdims=True)
    i1 = jnp.min(jnp.where(is_exp & (le == t1), lane, big), axis=-1, keepdims=True)
    le2 = jnp.where(lane == i1, NEG, le)
    t2 = jnp.max(le2, axis=-1, keepdims=True)
    i2 = jnp.min(jnp.where(is_exp & (lane != i1) & (le2 == t2), lane, big), axis=-1, keepdims=True)
    e2 = jnp.exp(t2 - t1)
    w1 = p_grp / (1.0 + e2)
    w2 = w1 * e2
    gate_ref[...] = jnp.where(lane == i1, w1, jnp.where(lane == i2, w2, 0.0))


def _merge(oa, ob, oc, p, x, wb, wo, nf, wr, br, tm, precise):
    r = x.shape[0]
    rows = lambda i: (i, 0)
    const2 = lambda i: (0, 0)
    return pl.pallas_call(
        functools.partial(_merge_kernel, precise=precise),
        grid=(r // tm,),
        in_specs=[
            pl.BlockSpec((tm, BRANCH_W), rows),
            pl.BlockSpec((tm, BRANCH_W), rows),
            pl.BlockSpec((tm, BRANCH_W), rows),
            pl.BlockSpec((tm, N_BRANCH * D_MODEL), rows),
            pl.BlockSpec((tm, D_MODEL), rows),
            pl.BlockSpec((N_BRANCH, BRANCH_W, D_MODEL), lambda i: (0, 0, 0)),
            pl.BlockSpec((D_MODEL, D_MODEL), const2),
            pl.BlockSpec((1, D_MODEL), const2),
            pl.BlockSpec((D_MODEL, LANES), const2),
            pl.BlockSpec((1, LANES), const2),
        ],
        out_specs=[
            pl.BlockSpec((tm, D_MODEL), rows),
            pl.BlockSpec((tm, D_MODEL), rows),
            pl.BlockSpec((tm, LANES), rows),
        ],
        out_shape=[jax.ShapeDtypeStruct((r, D_MODEL), F32), jax.ShapeDtypeStruct((r, D_MODEL), BF16),
                   jax.ShapeDtypeStruct((r, LANES), F32)],
        compiler_params=_params(1),
        name="merge_split" if precise else "merge",
    )(oa, ob, oc, p, x, wb, wo, nf, wr, br)


def _moe_kernel(xn_ref, gate_ref, x1_ref, w1_ref, w3_ref, w2_ref, nfin_ref, *out_refs, final):
    x2_ref = out_refs[0]
    e = pl.program_id(1)

    @pl.when(e == 0)
    def _():
        x2_ref[...] = x1_ref[...]

    xn = xn_ref[...]
    lane = lax.broadcasted_iota(jnp.int32, gate_ref.shape, 1)
    g_e = jnp.sum(jnp.where(lane == LANE_EXPERT + e, gate_ref[...], 0.0), axis=-1, keepdims=True)
    h = _silu(_dot(xn, w1_ref[0])) * _dot(xn, w3_ref[0])
    x2_ref[...] += _dot(h * g_e, w2_ref[0])

    if final:
        @pl.when(e == pl.num_programs(1) - 1)
        def _():
            out_refs[1][...] = _rms(x2_ref[...], nfin_ref[...])


def _moe(xn, gate, x1, w1, w3, w2, nfin, tm, final):
    r = x1.shape[0]
    rows = lambda i, e: (i, 0)
    out_specs = [pl.BlockSpec((tm, D_MODEL), rows)]
    out_shape = [jax.ShapeDtypeStruct((r, D_MODEL), F32)]
    if final:
        out_specs.append(pl.BlockSpec((tm, D_MODEL), rows))
        out_shape.append(jax.ShapeDtypeStruct((r, D_MODEL), F32))
    return pl.pallas_call(
        functools.partial(_moe_kernel, final=final),
        grid=(r // tm, N_EXPERTS),
        in_specs=[
            pl.BlockSpec((tm, D_MODEL), rows),
            pl.BlockSpec((tm, LANES), rows),
            pl.BlockSpec((tm, D_MODEL), rows),
            pl.BlockSpec((1, D_MODEL, D_FF), lambda i, e: (e, 0, 0)),
            pl.BlockSpec((1, D_MODEL, D_FF), lambda i, e: (e, 0, 0)),
            pl.BlockSpec((1, D_FF, D_MODEL), lambda i, e: (e, 0, 0)),
            pl.BlockSpec((1, D_MODEL), lambda i, e: (0, 0)),
        ],
        out_specs=out_specs,
        out_shape=out_shape,
        compiler_params=_params(2),
        name="moe",
    )(xn, gate, x1, w1, w3, w2, nfin)


def _rope_tables(pos):
    half = D_HEAD // 2
    inv = jnp.asarray((ROPE_BASE ** (-np.arange(half, dtype=np.float64) / half)).astype(np.float32))
    ang = pos.astype(F32)[:, None] * inv[None, :]
    cos, sin = jnp.cos(ang), jnp.sin(ang)
    return jnp.concatenate([cos, cos], axis=-1), jnp.concatenate([-sin, sin], axis=-1)


def _retention_tables(t_eff):
    idx = jnp.arange(CHUNK, dtype=F32)
    log_gamma = jnp.log(1.0 - jnp.exp2(-5.0 - jnp.arange(N_HEADS, dtype=F32)))
    incl = idx[:, None] >= idx[None, :]
    rel = jnp.where(incl, idx[:, None] - idx[None, :], 0.0)
    dmat = jnp.where(incl, jnp.exp(rel[None] * log_gamma[:, None, None]), 0.0)
    cross = jnp.exp((idx + 1.0)[:, None] * log_gamma[None, :])
    kdec = jnp.exp((t_eff - 1.0 - idx)[:, None] * log_gamma[None, :])
    tot = jnp.broadcast_to(jnp.exp(t_eff * log_gamma)[None, :], (CHUNK, N_HEADS))
    rtab = jnp.concatenate([cross, kdec, tot, jnp.zeros((CHUNK, LANES - 3 * N_HEADS), F32)], axis=1)
    return dmat, rtab


def _lanes_row(pairs):
    row = jnp.zeros((LANES,), F32)
    for start, vals in pairs:
        row = row.at[start:start + vals.shape[0]].set(vals.astype(F32))
    return row


def _pick(n, candidates):
    for c in candidates:
        if n % c == 0:
            return c
    raise ValueError(f"no tile in {candidates} divides {n}")


def kernel(x_prompt, x_sample, cache_k, cache_v, cache_logf, page_table, state_delta, state_conv, state_ret,
           meta_tokens, norm_mix, norm_ffn, norm_final, w_in, conv_w, a_log, dt_bias, a_norm, b_fbias, c_norm,
           w_branch, w_out, w_router_group, b_router_group, w_router_expert, b_router_expert, w1, w3, w2):
    nb, seq, _ = x_prompt.shape
    ns, ts, _ = x_sample.shape
    depth, n_pool = cache_k.shape[:2]
    n_pages = page_table.shape[1]
    assert seq % CHUNK == 0 and ts <= SAMPLE_ROWS and cache_k.shape[2] == CHUNK
    lp = CHUNK + seq
    n_chunks = lp // CHUNK
    rp = nb * lp
    rs = ns * SAMPLE_ROWS
    tm_p = _pick(rp, (512, 256, 128))
    tm_moe = _pick(rp, (768, 512, 256, 128))
    tm_s = _pick(rs, (256, 128, 64, 32, 16, 8))

    xp = jnp.concatenate([jnp.zeros((nb, PAD_FRONT, D_MODEL), F32),
                          jnp.broadcast_to(meta_tokens[None], (nb, N_META, D_MODEL)), x_prompt],
                         axis=1).reshape(rp, D_MODEL)
    xs = jnp.concatenate([x_sample, jnp.zeros((ns, SAMPLE_ROWS - ts, D_MODEL), F32)], axis=1).reshape(rs, D_MODEL)

    cos_p, sin_p = _rope_tables(jnp.arange(lp) - PAD_FRONT)
    cos_s, sin_s = _rope_tables(n_pages * CHUNK + jnp.arange(CHUNK))
    dmat_p, rtab_p = _retention_tables(float(CHUNK))
    dmat_s, rtab_s = _retention_tables(float(ts))

    assert (ts * N_HEADS) % 8 == 0
    n_cache = depth * n_pool
    cache_k2 = cache_k.reshape(n_cache * CHUNK * N_HEADS, D_HEAD)
    cache_v2 = cache_v.reshape(n_cache * CHUNK * N_HEADS, D_HEAD)
    c_page, tot_page = _page_cumsum(cache_logf.reshape(n_cache, CHUNK * N_HEADS))
    c_page = c_page.reshape(n_cache, 1, CHUNK * N_HEADS)
    tot_page = tot_page.reshape(n_cache, 1, CHUNK * N_HEADS)
    tb = _pick(lp, (384, 256, 128))
    par_p = _pick(nb, (PAR_PROMPT, 1))
    par_s = _pick(ns, (PAR_SAMPLE, 2, 1))
    ppg = _pick(n_pages, (16, 8, 4, 2, 1))

    def sample_heads(a):
        return a.reshape(ns, SAMPLE_ROWS, N_HEADS, D_HEAD)[:, :ts].reshape(ns, ts * N_HEADS, D_HEAD)

    def prompt_rows(a):
        return a.reshape(nb, lp, a.shape[-1])[:, PAD_FRONT:]

    def sample_rows(a):
        return a.reshape(ns, SAMPLE_ROWS, a.shape[-1])[:, :ts]

    def col(a, c, w=BRANCH_W):
        return a[:, c * BRANCH_W:c * BRANCH_W + w]

    outs_p = [[] for _ in range(6)]
    outs_s = [[] for _ in range(6)]
    yp = ys = None
    for l in range(depth):
        final = l == depth - 1
        hp = not final
        wdt = F32 if hp else BF16
        w_main, w_small = _reorder_w_in(w_in, l, wdt)
        g_mix = norm_mix[l][None]
        p_p, ps_p = _proj(xp, g_mix, w_main, w_small, tm_p, 512, hp, F32 if hp else BF16)
        p_s, ps_s = _proj(xs, g_mix, w_main, w_small, tm_s, 512, hp)

        avec = jnp.zeros((8, LANES), F32)
        avec = avec.at[0].set(_lanes_row([(LANE_ALPHA, a_log[l])]))
        avec = avec.at[1].set(_lanes_row([(LANE_ALPHA, dt_bias[l])]))
        avec = avec.at[2].set(_lanes_row([(LANE_F, b_fbias[l])]))
        shared = (conv_w[l], avec, a_norm[l][None], c_norm[l][None])
        oa_p, oc_p, lf_p, cc_p, sd_p, sr_p = _seq(
            p_p, ps_p, cos_p, sin_p, *shared, dmat_p, rtab_p, n_seq=nb, n_chunks=n_chunks, tc_in=CHUNK,
            valid_hi=CHUNK, front_pad=PAD_FRONT, precise=hp, n_par=par_p)
        cb0 = jnp.pad(state_conv[l], ((0, 0), (8 - (CONV_W - 1), 0), (0, 0)))
        oa_s, oc_s, lf_s, cc_s, sd_s, sr_s = _seq(
            p_s, ps_s, cos_s, sin_s, *shared, dmat_s, rtab_s, n_seq=ns, n_chunks=1, tc_in=SAMPLE_ROWS,
            valid_hi=ts, front_pad=0, precise=hp, n_par=par_s, init=(state_delta[l], state_ret[l], cb0))

        ct = jnp.pad(jnp.swapaxes(cc_p[:, LANE_F:LANE_F + N_HEADS].reshape(nb, lp, N_HEADS), 1, 2),
                     ((0, 0), (0, 8 - N_HEADS), (0, 0)))
        ob_p = _fox_prompt(p_p, ct, nb, lp, tb, hp)
        cn = cc_s[:, LANE_F:LANE_F + N_HEADS].reshape(ns, SAMPLE_ROWS, N_HEADS)[:, :ts].reshape(ns, 1, ts * N_HEADS)
        cn = jnp.pad(cn, ((0, 0), (0, 0), (0, CHUNK - ts * N_HEADS)))
        o_s = _fox_sample(page_table, sample_heads(col(p_s, COL_BQ)), sample_heads(col(p_s, COL_BK)),
                          sample_heads(col(p_s, COL_BV)), cn, cache_k2, cache_v2, c_page, tot_page,
                          layer=l, n_pool=n_pool, ppg=ppg, precise=hp)
        ob_s = jnp.pad(o_s.reshape(ns, ts, BRANCH_W), ((0, 0), (0, SAMPLE_ROWS - ts), (0, 0))).reshape(rs, BRANCH_W)

        wr = jnp.concatenate([w_router_group[l], w_router_expert[l],
                              jnp.zeros((D_MODEL, LANES - N_GROUPS - N_EXPERTS), F32)], axis=1)
        br = _lanes_row([(LANE_GROUP, b_router_group[l]), (LANE_EXPERT, b_router_expert[l].reshape(-1))])[None]
        nf = norm_ffn[l][None]
        wb, wo = w_branch[l].astype(wdt), w_out[l].astype(wdt)
        x1_p, xn_p, gate_p = _merge(oa_p, ob_p, oc_p, p_p, xp, wb, wo, nf, wr, br, min(tm_p, 256), hp)
        x1_s, xn_s, gate_s = _merge(oa_s, ob_s, oc_s, p_s, xs, wb, wo, nf, wr, br, tm_s, hp)

        ew1 = w1[l].reshape(N_EXPERTS, D_MODEL, D_FF).astype(BF16)
        ew3 = w3[l].reshape(N_EXPERTS, D_MODEL, D_FF).astype(BF16)
        ew2 = w2[l].reshape(N_EXPERTS, D_FF, D_MODEL).astype(BF16)
        res_p = _moe(xn_p, gate_p, x1_p, ew1, ew3, ew2, norm_final[None], tm_moe, final)
        res_s = _moe(xn_s, gate_s, x1_s, ew1, ew3, ew2, norm_final[None], tm_s, final)
        xp, xs = res_p[0], res_s[0]
        if final:
            yp, ys = res_p[1], res_s[1]

        for lst, rows_of, p, lf_a, sd, sr in ((outs_p, prompt_rows, p_p, lf_p, sd_p, sr_p),
                                              (outs_s, sample_rows, p_s, lf_s, sd_s, sr_s)):
            k_new = rows_of(col(p, COL_BK)).astype(F32)
            v_new = rows_of(col(p, COL_BV)).astype(F32)
            lst[0].append(k_new.reshape(k_new.shape[:2] + (N_HEADS, D_HEAD)))
            lst[1].append(v_new.reshape(v_new.shape[:2] + (N_HEADS, D_HEAD)))
            lst[2].append(rows_of(lf_a[:, LANE_F:LANE_F + N_HEADS]))
            lst[3].append(sd)
            lst[4].append(rows_of(col(p, COL_AQKV, A_QKV))[:, -(CONV_W - 1):].astype(F32))
            lst[5].append(sr)

    y_prompt = yp.reshape(nb, lp, D_MODEL)[:, CHUNK:]
    y_sample = ys.reshape(ns, SAMPLE_ROWS, D_MODEL)[:, :ts]
    return (y_prompt, y_sample, *(jnp.stack(a) for a in outs_p), *(jnp.stack(a) for a in outs_s))
```

```python
import functools

import jax
import jax.numpy as jnp
import numpy as np
from jax import lax
from jax.experimental import pallas as pl
from jax.experimental.pallas import tpu as pltpu

F32 = jnp.float32
BF16 = jnp.bfloat16

D_MODEL = 1024
N_META = 16
CHUNK = 128
PAD_FRONT = CHUNK - N_META
N_HEADS = 4
D_HEAD = 128
BRANCH_W = N_HEADS * D_HEAD
A_QKV = 3 * BRANCH_W
CONV_W = 4
N_BRANCH = 3
N_GROUPS = 4
EXP_PER_GROUP = 4
N_EXPERTS = N_GROUPS * EXP_PER_GROUP
D_FF = 512
ROPE_BASE = 10000.0
EPS = 1e-6
NEG = -1e30
SAMPLE_ROWS = 8
LANES = 128

COL_GATES = 0
COL_AQKV = 6
COL_AZ = 9
COL_BQ, COL_BK, COL_BV = 10, 11, 12
COL_CQ, COL_CK, COL_CV, COL_CG = 13, 14, 15, 16
N_MAIN = 17 * BRANCH_W
LANE_BETA, LANE_ALPHA, LANE_F = 0, 4, 8
LANE_GROUP, LANE_EXPERT = 0, 4

VMEM_LIMIT = 56 * 1024 * 1024
PAR_PROMPT, PAR_SAMPLE = 4, 4

NN = (((1,), (0,)), ((), ()))
NT = (((1,), (1,)), ((), ()))
TN = (((0,), (0,)), ((), ()))


def _split2(x):
    hi = x.astype(BF16)
    return hi, (x - hi.astype(F32)).astype(BF16)


def _dot(a, b, dims=NN, precise=False):
    def dg(x, y):
        return lax.dot_general(x, y, dims, preferred_element_type=F32)

    if not precise:
        a = a[0] if isinstance(a, tuple) else a.astype(BF16)
        b = b[0] if isinstance(b, tuple) else b.astype(BF16)
        return dg(a, b)
    a_hi, a_lo = a if isinstance(a, tuple) else _split2(a)
    b_hi, b_lo = b if isinstance(b, tuple) else _split2(b)
    return dg(a_hi, b_hi) + dg(a_hi, b_lo) + dg(a_lo, b_hi)


def _split3(x):
    x1 = x.astype(BF16)
    r1 = x - x1.astype(F32)
    x2 = r1.astype(BF16)
    x3 = (r1 - x2.astype(F32)).astype(BF16)
    return x1, x2, x3


def _sigmoid(x):
    return 1.0 / (1.0 + jnp.exp(-x))


def _silu(x):
    return x * _sigmoid(x)


def _softplus(x):
    return jnp.maximum(x, 0.0) + jnp.log1p(jnp.exp(-jnp.abs(x)))


def _rms(x, g):
    return x * lax.rsqrt(jnp.mean(x * x, axis=-1, keepdims=True) + EPS) * g


def _params(n_axes):
    return pltpu.CompilerParams(dimension_semantics=("arbitrary",) * n_axes, vmem_limit_bytes=VMEM_LIMIT)


SEGMENTS = ((0, A_QKV + 8 * BRANCH_W + 12, N_BRANCH * D_MODEL),
            (N_BRANCH * D_MODEL, 0, A_QKV + BRANCH_W),
            (N_BRANCH * D_MODEL + 4 * BRANCH_W, A_QKV + BRANCH_W + 8, 3 * BRANCH_W),
            (N_BRANCH * D_MODEL + 7 * BRANCH_W, A_QKV + 4 * BRANCH_W + 12, 4 * BRANCH_W))
OLD_SMALL_A = A_QKV + BRANCH_W
OLD_SMALL_F = A_QKV + 4 * BRANCH_W + 8


def _reorder_kernel(blk_ref, sh_ref, a_ref, b_ref, o_ref, *, shifts):
    j = pl.program_id(0)
    for shift in shifts:
        @pl.when(sh_ref[j] == shift)
        def _(shift=shift):
            a = a_ref[...]
            if shift:
                a = jnp.concatenate([a[:, shift:], b_ref[:, :shift]], axis=1)
            o_ref[...] = a.astype(o_ref.dtype)


def _small_kernel(a_ref, f_ref, o_ref):
    lane = lax.broadcasted_iota(jnp.int32, a_ref.shape, 1)
    o_ref[...] = jnp.where(lane < LANE_F, a_ref[...], jnp.where(lane < LANE_F + N_HEADS, f_ref[...], 0.0)
                           ).astype(o_ref.dtype)


def _reorder_w_in(w_in, layer, dtype):
    n_old = w_in.shape[2]
    last = (n_old - 1) // LANES
    blks, shifts = [], []
    for new0, old0, width in SEGMENTS:
        assert new0 % LANES == 0 and width % LANES == 0
        for t in range(width // LANES):
            blks.append(old0 // LANES + t)
            shifts.append(old0 % LANES)
    grid_spec = pltpu.PrefetchScalarGridSpec(
        num_scalar_prefetch=2,
        grid=(len(blks),),
        in_specs=[pl.BlockSpec((None, D_MODEL, LANES), lambda j, blk, sh: (layer, 0, blk[j])),
                  pl.BlockSpec((None, D_MODEL, LANES), lambda j, blk, sh: (layer, 0, jnp.minimum(blk[j] + 1, last)))],
        out_specs=pl.BlockSpec((D_MODEL, LANES), lambda j, blk, sh: (0, j)),
    )
    w_main = pl.pallas_call(
        functools.partial(_reorder_kernel, shifts=tuple(sorted(set(shifts)))),
        grid_spec=grid_spec,
        out_shape=jax.ShapeDtypeStruct((D_MODEL, N_MAIN), dtype),
        compiler_params=_params(1),
        name="reorder_w_in",
    )(jnp.asarray(blks, jnp.int32), jnp.asarray(shifts, jnp.int32), w_in, w_in)
    assert OLD_SMALL_A % LANES == LANE_BETA and OLD_SMALL_F % LANES == LANE_F
    w_small = pl.pallas_call(
        _small_kernel,
        grid=(1,),
        in_specs=[pl.BlockSpec((None, D_MODEL, LANES), lambda i: (layer, 0, OLD_SMALL_A // LANES)),
                  pl.BlockSpec((None, D_MODEL, LANES), lambda i: (layer, 0, OLD_SMALL_F // LANES))],
        out_specs=pl.BlockSpec((D_MODEL, LANES), lambda i: (0, 0)),
        out_shape=jax.ShapeDtypeStruct((D_MODEL, LANES), dtype),
        compiler_params=_params(1),
        name="reorder_w_small",
    )(w_in, w_in)
    return w_main, w_small


def _proj_kernel(x_ref, g_ref, w_ref, ws_ref, p_ref, ps_ref, xn_ref, *, tn, precise):
    j = pl.program_id(1)

    @pl.when(j == 0)
    def _():
        xn = _split2(_rms(x_ref[...], g_ref[...]))
        xn_ref[0] = xn[0]
        if precise:
            xn_ref[1] = xn[1]
        ps_ref[...] = _dot(xn, ws_ref[...], precise=precise)

    if precise:
        p_ref[...] = _dot((xn_ref[0], xn_ref[1]), w_ref[...], precise=True).astype(p_ref.dtype)
    else:
        p_ref[...] = _dot(xn_ref[0], w_ref[:, pl.ds(pl.multiple_of(j * tn, tn), tn)]).astype(p_ref.dtype)


def _proj(x, g, w_main, w_small, tm, tn, precise, out_dtype=F32):
    r = x.shape[0]
    w_spec = (pl.BlockSpec((D_MODEL, tn), lambda i, j: (0, j)) if precise else
              pl.BlockSpec((D_MODEL, N_MAIN), lambda i, j: (0, 0)))
    return pl.pallas_call(
        functools.partial(_proj_kernel, tn=tn, precise=precise),
        grid=(r // tm, N_MAIN // tn),
        in_specs=[pl.BlockSpec((tm, D_MODEL), lambda i, j: (i, 0)),
                  pl.BlockSpec((1, D_MODEL), lambda i, j: (0, 0)),
                  w_spec,
                  pl.BlockSpec((D_MODEL, LANES), lambda i, j: (0, 0))],
        out_specs=[pl.BlockSpec((tm, tn), lambda i, j: (i, j)),
                   pl.BlockSpec((tm, LANES), lambda i, j: (i, 0))],
        out_shape=[jax.ShapeDtypeStruct((r, N_MAIN), out_dtype), jax.ShapeDtypeStruct((r, LANES), F32)],
        scratch_shapes=[pltpu.VMEM((2 if precise else 1, tm, D_MODEL), BF16)],
        compiler_params=_params(2),
        name="proj_split" if precise else "proj",
    )(x, g, w_main, w_small)


def _delta_heads(qs, ks, vs, betas, gcols, grows, states, incl, strict, precise):
    mm = functools.partial(_dot, precise=precise)
    n = range(len(qs))
    egs = [jnp.exp(gcols[i]) for i in n]
    decays = [jnp.where(incl, jnp.exp(jnp.where(incl, gcols[i] - grows[i], 0.0)), 0.0) for i in n]
    kbs = [ks[i] * betas[i] for i in n]
    ms = [jnp.where(strict, mm(kbs[i], ks[i], NT) * decays[i], 0.0) for i in n]
    rs = [-m for m in ms]
    pows = [_dot(r, r) for r in rs]
    for it in range(6):
        rps = [_dot(rs[i], pows[i]) for i in n]
        if it < 5:
            nxt = [_dot(pows[i], pows[i]) for i in n]
        rs = [rs[i] + pows[i] + rps[i] for i in n]
        if it < 5:
            pows = nxt
    for _ in range(2 if precise else 1):
        es = [-(ms[i] + rs[i] + _dot(ms[i], rs[i], precise=True)) for i in n]
        res = [_dot(rs[i], es[i]) for i in n]
        rs = [rs[i] + es[i] + res[i] for i in n]
    vbs = [vs[i] * betas[i] for i in n]
    kbes = [kbs[i] * egs[i] for i in n]
    us = [vbs[i] + mm(rs[i], vbs[i]) for i in n]
    ws = [kbes[i] + mm(rs[i], kbes[i]) for i in n]
    attns = [mm(qs[i], ks[i], NT) * decays[i] for i in n]
    v_news = [us[i] - mm(ws[i], states[i]) for i in n]
    outs = [mm(qs[i] * egs[i], states[i]) + mm(attns[i], v_news[i]) for i in n]
    g_lasts = [g[CHUNK - 1:CHUNK, :] for g in gcols]
    new_states = [states[i] * jnp.exp(g_lasts[i]) + mm(ks[i] * jnp.exp(g_lasts[i] - gcols[i]), v_news[i], TN)
                  for i in n]
    return outs, new_states


def _seq_kernel(*refs, n_par, tc_in, valid_hi, front_pad, has_init, precise):
    if has_init:
        (aqkv_ref, az_ref, cq_ref, ck_ref, cv_ref, cg_ref, sm_ref, cos_ref, sin_ref,
         convw_ref, avec_ref, anorm_ref, cnorm_ref, dmat_ref, rtab_ref,
         sd0_ref, sr0_ref, cb0_ref,
         oa_ref, oc_ref, lf_ref, cc_ref, sd_ref, sr_ref, xbuf, cbuf, carry) = refs
    else:
        (aqkv_ref, az_ref, cq_ref, ck_ref, cv_ref, cg_ref, sm_ref, cos_ref, sin_ref,
         convw_ref, avec_ref, anorm_ref, cnorm_ref, dmat_ref, rtab_ref,
         oa_ref, oc_ref, lf_ref, cc_ref, sd_ref, sr_ref, xbuf, cbuf, carry) = refs
    mm = functools.partial(_dot, precise=precise)
    c = pl.program_id(1)

    def ext(ref, s, cols=None):
        v = (ref[s] if cols is None else ref[s, :, cols]).astype(F32)
        if tc_in == CHUNK:
            return v
        return jnp.concatenate([v, jnp.zeros((CHUNK - tc_in, v.shape[1]), v.dtype)], axis=0)

    @pl.when(c == 0)
    def _():
        carry[...] = jnp.zeros_like(carry)
        if has_init:
            xbuf[:, 0:8, :] = cb0_ref[...]
            sd_ref[...] = sd0_ref[...]
            sr_ref[...] = sr0_ref[...]
        else:
            xbuf[:, 0:8, :] = jnp.zeros((n_par, 8, A_QKV), F32)
            sd_ref[...] = jnp.zeros_like(sd_ref)
            sr_ref[...] = jnp.zeros_like(sr_ref)

    rows = lax.broadcasted_iota(jnp.int32, (CHUNK, LANES), 0)
    lane = lax.broadcasted_iota(jnp.int32, (CHUNK, LANES), 1)
    lo = jnp.where(c == 0, front_pad, 0)
    valid = (rows >= lo) & (rows < valid_hi)
    incl = rows >= lane
    strict = rows > lane
    is_g = (lane >= LANE_ALPHA) & (lane < LANE_ALPHA + N_HEADS)
    is_f = (lane >= LANE_F) & (lane < LANE_F + N_HEADS)
    tril = jnp.where(incl, 1.0, 0.0).astype(BF16)
    av = avec_ref[...]
    cos2 = cos_ref[...]
    sin2 = sin_ref[...]
    rtab = rtab_ref[...]

    heads = range(N_HEADS)
    hsl = [slice(h * D_HEAD, (h + 1) * D_HEAD) for h in heads]
    chains = [(s, h) for s in range(n_par) for h in heads]
    qs, ks, vs, betas, gcols, grows = [], [], [], [], [], []
    for s in range(n_par):
        for j in range(A_QKV // LANES):
            cols = slice(j * LANES, (j + 1) * LANES)
            xbuf[s, 8:8 + CHUNK, cols] = ext(aqkv_ref, s, cols)
            acc = xbuf[s, 5:5 + CHUNK, cols] * convw_ref[0:1, cols]
            for i in range(1, CONV_W):
                acc = acc + xbuf[s, 5 + i:5 + i + CHUNK, cols] * convw_ref[i:i + 1, cols]
            cbuf[s, :, cols] = jnp.where(valid, _silu(acc), 0.0)
            xbuf[s, 0:8, cols] = xbuf[s, CHUNK:CHUNK + 8, cols]

        sm = ext(sm_ref, s)
        g_all = -jnp.exp(av[0:1, :]) * _softplus(sm + av[1:2, :])
        lf_all = -_softplus(-(sm + av[2:3, :]))
        z = jnp.where(valid & is_g, g_all, jnp.where(valid & is_f, lf_all, 0.0))
        cum = sum(jnp.dot(tril, part, preferred_element_type=F32) for part in _split3(z))
        cum_t = cum.T
        cc = cum + carry[s]
        carry[s] = cc[CHUNK - 1:CHUNK, :]
        lf_ref[s] = z[:tc_in]
        cc_ref[s] = cc[:tc_in]

        beta_all = _sigmoid(sm)
        for h in heads:
            q = cbuf[s, :, hsl[h]]
            k = cbuf[s, :, BRANCH_W + h * D_HEAD:BRANCH_W + (h + 1) * D_HEAD]
            qs.append(q * lax.rsqrt(jnp.sum(q * q, axis=-1, keepdims=True) + EPS) * (D_HEAD ** -0.5))
            ks.append(k * lax.rsqrt(jnp.sum(k * k, axis=-1, keepdims=True) + EPS))
            vs.append(cbuf[s, :, 2 * BRANCH_W + h * D_HEAD:2 * BRANCH_W + (h + 1) * D_HEAD])
            betas.append(beta_all[:, LANE_BETA + h:LANE_BETA + h + 1])
            gcols.append(cum[:, LANE_ALPHA + h:LANE_ALPHA + h + 1])
            grows.append(cum_t[LANE_ALPHA + h:LANE_ALPHA + h + 1, :])

    outs, new_states = _delta_heads(qs, ks, vs, betas, gcols, grows, [sd_ref[s, h] for s, h in chains],
                                    incl, strict, precise)
    for i, (s, h) in enumerate(chains):
        sd_ref[s, h] = new_states[i]
        oa = _rms(outs[i], anorm_ref[...]) * _silu(ext(az_ref, s, hsl[h]))
        oa_ref[s, :, hsl[h]] = oa[:tc_in]

    qcs = [ext(cq_ref, s, hsl[h]) for s, h in chains]
    kcs = [ext(ck_ref, s, hsl[h]) for s, h in chains]
    vcs = [ext(cv_ref, s, hsl[h]) for s, h in chains]
    qcs = [q * cos2 + pltpu.roll(q, D_HEAD // 2, 1) * sin2 for q in qcs]
    kcs = [(k * cos2 + pltpu.roll(k, D_HEAD // 2, 1) * sin2) * (D_HEAD ** -0.5) for k in kcs]
    srs = [sr_ref[s, h] for s, h in chains]
    scores = [mm(qcs[i], kcs[i], NT) * dmat_ref[h] for i, (s, h) in enumerate(chains)]
    crosses = [mm(qcs[i] * rtab[:, h:h + 1], srs[i]) for i, (s, h) in enumerate(chains)]
    kvs = [mm(kcs[i] * rtab[:, 4 + h:5 + h], vcs[i], TN) for i, (s, h) in enumerate(chains)]
    inners = [mm(scores[i], vcs[i]) for i in range(len(chains))]
    for i, (s, h) in enumerate(chains):
        sr_ref[s, h] = srs[i] * rtab[0:1, 8 + h:9 + h] + kvs[i]
        oc = _rms(inners[i] + crosses[i], cnorm_ref[...]) * _silu(ext(cg_ref, s, hsl[h]))
        oc_ref[s, :, hsl[h]] = oc[:tc_in]


def _seq(p, ps, cos2, sin2, convw, avec, anorm, cnorm, dmat, rtab, *, n_seq, n_chunks, tc_in,
         valid_hi, front_pad, precise, n_par, init=None):
    r = p.shape[0]
    lp = n_chunks * tc_in
    p3 = p.reshape(n_seq, lp, N_MAIN)
    ps3 = ps.reshape(n_seq, lp, LANES)

    def rows(width, col):
        return pl.BlockSpec((n_par, tc_in, width), lambda n, c: (n, c, col))

    const2 = lambda n, c: (0, 0)
    state_spec = pl.BlockSpec((n_par, N_HEADS, D_HEAD, D_HEAD), lambda n, c: (n, 0, 0, 0))
    in_specs = [
        rows(A_QKV, COL_AQKV // 3),
        rows(BRANCH_W, COL_AZ),
        rows(BRANCH_W, COL_CQ),
        rows(BRANCH_W, COL_CK),
        rows(BRANCH_W, COL_CV),
        rows(BRANCH_W, COL_CG),
        rows(LANES, 0),
        pl.BlockSpec((CHUNK, D_HEAD), lambda n, c: (c, 0)),
        pl.BlockSpec((CHUNK, D_HEAD), lambda n, c: (c, 0)),
        pl.BlockSpec((CONV_W, A_QKV), const2),
        pl.BlockSpec((8, LANES), const2),
        pl.BlockSpec((1, D_HEAD), const2),
        pl.BlockSpec((1, D_HEAD), const2),
        pl.BlockSpec((N_HEADS, CHUNK, CHUNK), lambda n, c: (0, 0, 0)),
        pl.BlockSpec((CHUNK, LANES), const2),
    ]
    args = [p3, p3, p3, p3, p3, p3, ps3, cos2, sin2, convw, avec, anorm, cnorm, dmat, rtab]
    if init is not None:
        in_specs += [state_spec, state_spec, pl.BlockSpec((n_par, 8, A_QKV), lambda n, c: (n, 0, 0))]
        args += list(init)
    out_specs = [rows(BRANCH_W, 0), rows(BRANCH_W, 0), rows(LANES, 0), rows(LANES, 0), state_spec, state_spec]
    out_shape = [
        jax.ShapeDtypeStruct((n_seq, lp, BRANCH_W), F32),
        jax.ShapeDtypeStruct((n_seq, lp, BRANCH_W), F32),
        jax.ShapeDtypeStruct((n_seq, lp, LANES), F32),
        jax.ShapeDtypeStruct((n_seq, lp, LANES), F32),
        jax.ShapeDtypeStruct((n_seq, N_HEADS, D_HEAD, D_HEAD), F32),
        jax.ShapeDtypeStruct((n_seq, N_HEADS, D_HEAD, D_HEAD), F32),
    ]
    kern = functools.partial(_seq_kernel, n_par=n_par, tc_in=tc_in, valid_hi=valid_hi, front_pad=front_pad,
                             has_init=init is not None, precise=precise)
    oa, oc, lf, cc, sd, sr = pl.pallas_call(
        kern,
        grid=(n_seq // n_par, n_chunks),
        in_specs=in_specs,
        out_specs=out_specs,
        out_shape=out_shape,
        scratch_shapes=[pltpu.VMEM((n_par, CHUNK + 8, A_QKV), F32), pltpu.VMEM((n_par, CHUNK, A_QKV), F32),
                        pltpu.VMEM((n_par, 1, LANES), F32)],
        compiler_params=_params(2),
        name=("seq_prompt" if init is None else "seq_sample") + ("_split" if precise else ""),
    )(*args)
    return (oa.reshape(r, BRANCH_W), oc.reshape(r, BRANCH_W), lf.reshape(r, LANES), cc.reshape(r, LANES), sd, sr)


def _fox_prompt_kernel(q_ref, k_ref, v_ref, ct_ref, o_ref, m_ref, l_ref, acc_ref, *, tb, precise):
    mm = functools.partial(_dot, precise=precise)
    i = pl.program_id(1)
    j = pl.program_id(2)

    @pl.when(j == 0)
    def _():
        m_ref[...] = jnp.full_like(m_ref, NEG)
        l_ref[...] = jnp.zeros_like(l_ref)
        acc_ref[...] = jnp.zeros_like(acc_ref)

    def block(masked):
        if masked:
            qpos = i * tb + lax.broadcasted_iota(jnp.int32, (tb, tb), 0)
            kpos = j * tb + lax.broadcasted_iota(jnp.int32, (tb, tb), 1)
            ok = (kpos <= qpos) & (kpos >= PAD_FRONT)
        for h in range(N_HEADS):
            hs = slice(h * D_HEAD, (h + 1) * D_HEAD)
            s = mm(q_ref[:, hs].astype(F32) * (D_HEAD ** -0.5), k_ref[:, hs], NT) - ct_ref[0, h:h + 1, :]
            if masked:
                s = jnp.where(ok, s, NEG)
            m_old = m_ref[h][:, 0:1]
            m_new = jnp.maximum(m_old, jnp.max(s, axis=-1, keepdims=True))
            alpha = jnp.exp(m_old - m_new)
            if precise:
                p = jnp.exp(s - m_new)
                mass = jnp.sum(p, axis=-1, keepdims=True)
            else:
                p = jnp.exp((s - m_new).astype(BF16))
                mass = jnp.dot(p, jnp.ones((tb, LANES), BF16), preferred_element_type=F32)
            l_ref[h] = alpha * l_ref[h] + mass
            acc_ref[h] = alpha * acc_ref[h] + mm(p, v_ref[:, hs])
            m_ref[h] = jnp.broadcast_to(m_new, (tb, LANES))

    needs_mask = (j == i) | (j == 0)
    pl.when(needs_mask)(functools.partial(block, True))
    pl.when((j < i) & (j > 0))(functools.partial(block, False))

    @pl.when(j == i)
    def _():
        qvalid = (i * tb + lax.broadcasted_iota(jnp.int32, (tb, D_HEAD), 0)) >= PAD_FRONT
        for h in range(N_HEADS):
            o = acc_ref[h] / l_ref[h][:, 0:1]
            o_ref[:, h * D_HEAD:(h + 1) * D_HEAD] = jnp.where(qvalid, o, 0.0)


def _fox_prompt(p, ct, n_seq, lp, tb, precise):
    r = p.shape[0]
    nb = lp // tb
    return pl.pallas_call(
        functools.partial(_fox_prompt_kernel, tb=tb, precise=precise),
        grid=(n_seq, nb, nb),
        in_specs=[
            pl.BlockSpec((tb, BRANCH_W), lambda n, i, j: (n * nb + i, COL_BQ)),
            pl.BlockSpec((tb, BRANCH_W), lambda n, i, j: (n * nb + jnp.minimum(i, j), COL_BK)),
            pl.BlockSpec((tb, BRANCH_W), lambda n, i, j: (n * nb + jnp.minimum(i, j), COL_BV)),
            pl.BlockSpec((1, 8, tb), lambda n, i, j: (n, 0, jnp.minimum(i, j))),
        ],
        out_specs=pl.BlockSpec((tb, BRANCH_W), lambda n, i, j: (n * nb + i, 0)),
        out_shape=jax.ShapeDtypeStruct((r, BRANCH_W), F32),
        scratch_shapes=[pltpu.VMEM((N_HEADS, tb, LANES), F32), pltpu.VMEM((N_HEADS, tb, LANES), F32),
                        pltpu.VMEM((N_HEADS, tb, D_HEAD), F32)],
        compiler_params=_params(3),
        name="fox_prompt",
    )(p, p, p, ct)


def _page_cumsum_kernel(lf_ref, c_ref, tot_ref):
    n = CHUNK * N_HEADS
    ri = lax.broadcasted_iota(jnp.int32, (n, n), 0)
    ci = lax.broadcasted_iota(jnp.int32, (n, n), 1)
    same_head = (ri % N_HEADS) == (ci % N_HEADS)
    upper = jnp.where(same_head & (ri <= ci), 1.0, 0.0).astype(BF16)
    total = jnp.where(same_head, 1.0, 0.0).astype(BF16)
    parts = _split3(lf_ref[...])
    c_ref[...] = sum(jnp.dot(x, upper, preferred_element_type=F32) for x in parts)
    tot_ref[...] = sum(jnp.dot(x, total, preferred_element_type=F32) for x in parts)


def _page_cumsum(lf):
    n_pages, n = lf.shape
    tp = _pick(n_pages, (256, 128, 64, 32, 16, 8, n_pages))
    spec = pl.BlockSpec((tp, n), lambda i: (i, 0))
    return pl.pallas_call(
        _page_cumsum_kernel,
        grid=(n_pages // tp,),
        in_specs=[spec],
        out_specs=[spec, spec],
        out_shape=[jax.ShapeDtypeStruct((n_pages, n), F32)] * 2,
        compiler_params=_params(1),
        name="page_cumsum",
    )(lf)


def _fox_sample_kernel(*refs, ppg, n_q, precise):
    q_ref, kn_ref, vn_ref, cn_ref = refs[1:5]
    k_refs = refs[5:5 + ppg]
    v_refs = refs[5 + ppg:5 + 2 * ppg]
    c_refs = refs[5 + 2 * ppg:5 + 3 * ppg]
    tot_refs = refs[5 + 3 * ppg:5 + 4 * ppg]
    o_ref, m_ref, l_ref, acc_ref, carry = refs[5 + 4 * ppg:]
    mm = functools.partial(_dot, precise=precise)
    g = pl.program_id(1)
    n_keys = CHUNK * N_HEADS

    @pl.when(g == 0)
    def _():
        m_ref[...] = jnp.full_like(m_ref, NEG)
        l_ref[...] = jnp.zeros_like(l_ref)
        acc_ref[...] = jnp.zeros_like(acc_ref)
        carry[...] = jnp.zeros_like(carry)

    row = lax.broadcasted_iota(jnp.int32, (n_q, n_keys), 0)
    lane = lax.broadcasted_iota(jnp.int32, (n_q, n_keys), 1)
    same_head = (row % N_HEADS) == (lane % N_HEADS)
    scale = D_HEAD ** -0.5
    q = _split2(q_ref[0])

    def update(scores, values):
        top = scores[0]
        for s in scores[1:]:
            top = jnp.maximum(top, s)
        m_old = m_ref[:, 0:1]
        m_new = jnp.maximum(m_old, jnp.max(top, axis=-1, keepdims=True))
        alpha = jnp.exp(m_old - m_new)
        probs = [jnp.exp(s - m_new) for s in scores]
        mass = probs[0]
        for p in probs[1:]:
            mass = mass + p
        pv = mm(probs[0], values[0])
        for p, v in zip(probs[1:], values[1:]):
            pv = pv + mm(p, v)
        l_ref[...] = alpha * l_ref[...] + jnp.sum(mass, axis=-1, keepdims=True)
        acc_ref[...] = alpha * acc_ref[...] + pv
        m_ref[...] = jnp.broadcast_to(m_new, m_ref.shape)

    offset = carry[...]
    scores = []
    for pg in range(ppg):
        s = mm(q, k_refs[pg][...], NT) * scale - (offset + c_refs[pg][0])
        scores.append(jnp.where(same_head, s, NEG))
        offset = offset + tot_refs[pg][0]
    carry[...] = offset
    update(scores, [v_refs[pg][...] for pg in range(ppg)])

    @pl.when(g == pl.num_programs(1) - 1)
    def _():
        zpad = jnp.zeros((CHUNK - n_q, D_HEAD), F32)
        kn = jnp.concatenate([kn_ref[0], zpad], axis=0)
        vn = jnp.concatenate([vn_ref[0], zpad], axis=0)
        bias = carry[:, 0:CHUNK] + cn_ref[0]
        s = mm(q, kn, NT) * scale - bias
        r = lax.broadcasted_iota(jnp.int32, (n_q, CHUNK), 0)
        c = lax.broadcasted_iota(jnp.int32, (n_q, CHUNK), 1)
        ok = ((r % N_HEADS) == (c % N_HEADS)) & (c // N_HEADS <= r // N_HEADS) & (c < n_q)
        update([jnp.where(ok, s, NEG)], [vn])
        o_ref[0] = acc_ref[...] / l_ref[:, 0:1]


def _fox_sample(page_table, q, kn, vn, cn, cache_k, cache_v, c_page, tot_page, *, layer, n_pool, ppg, precise):
    n_seq, n_q, _ = q.shape
    n_pages = page_table.shape[1]
    n_keys = CHUNK * N_HEADS
    base = layer * n_pool
    pt = page_table.reshape(-1)

    def page(pg, ndim):
        return lambda n, g, pt_ref: (base + pt_ref[n * n_pages + g * ppg + pg],) + (0,) * (ndim - 1)

    seq3 = lambda n, g, pt_ref: (n, 0, 0)
    in_specs = [pl.BlockSpec((1, n_q, D_HEAD), seq3)] * 3 + [pl.BlockSpec((1, 1, CHUNK), seq3)]
    in_specs += [pl.BlockSpec((n_keys, D_HEAD), page(pg, 2)) for _ in range(2) for pg in range(ppg)]
    in_specs += [pl.BlockSpec((1, 1, n_keys), page(pg, 3)) for _ in range(2) for pg in range(ppg)]
    grid_spec = pltpu.PrefetchScalarGridSpec(
        num_scalar_prefetch=1,
        grid=(n_seq, n_pages // ppg),
        in_specs=in_specs,
        out_specs=pl.BlockSpec((1, n_q, D_HEAD), seq3),
        scratch_shapes=[pltpu.VMEM((n_q, LANES), F32), pltpu.VMEM((n_q, LANES), F32),
                        pltpu.VMEM((n_q, D_HEAD), F32), pltpu.VMEM((1, n_keys), F32)],
    )
    return pl.pallas_call(
        functools.partial(_fox_sample_kernel, ppg=ppg, n_q=n_q, precise=precise),
        grid_spec=grid_spec,
        out_shape=jax.ShapeDtypeStruct((n_seq, n_q, D_HEAD), F32),
        compiler_params=_params(2),
        name="fox_sample_split" if precise else "fox_sample",
    )(pt, q, kn, vn, cn, *([cache_k] * ppg), *([cache_v] * ppg), *([c_page] * ppg), *([tot_page] * ppg))


def _merge_kernel(oa_ref, ob_ref, oc_ref, gates_ref, x_ref, wb_ref, wo_ref, nf_ref, wr_ref, br_ref,
                  x1_ref, xn_ref, gate_ref, *, precise):
    mm = functools.partial(_dot, precise=precise)
    tm = x_ref.shape[0]
    merged = None
    for b, o_ref in enumerate((oa_ref, ob_ref, oc_ref)):
        term = mm(o_ref[...], wb_ref[b]) * _sigmoid(gates_ref[:, b * D_MODEL:(b + 1) * D_MODEL].astype(F32))
        merged = term if merged is None else merged + term
    x1 = x_ref[...] + mm(merged, wo_ref[...])
    x1_ref[...] = x1
    xn = _split2(_rms(x1, nf_ref[...]))
    xn_ref[...] = xn[0]
    logits = _dot(xn, wr_ref[...], precise=True) + br_ref[...]

    lane = lax.broadcasted_iota(jnp.int32, (tm, LANES), 1).astype(F32)
    big = float(LANES)
    is_grp = lane < float(N_GROUPS)
    mx = jnp.max(jnp.where(is_grp, logits, NEG), axis=-1, keepdims=True)
    p_grp = 1.0 / jnp.sum(jnp.where(is_grp, jnp.exp(jnp.where(is_grp, logits - mx, 0.0)), 0.0),
                          axis=-1, keepdims=True)
    g_star = jnp.min(jnp.where(is_grp & (logits == mx), lane, big), axis=-1, keepdims=True)
    base = float(LANE_EXPERT) + float(EXP_PER_GROUP) * g_star
    is_exp = (lane >= base) & (lane < base + float(EXP_PER_GROUP))
    le = jnp.where(is_exp, logits, NEG)
    t1 = jnp.max(le, axis=-1, keepdims=True)
    i1 = jnp.min(jnp.where(is_exp & (le == t1), lane, big), axis=-1, keepdims=True)
    le2 = jnp.where(lane == i1, NEG, le)
    t2 = jnp.max(le2, axis=-1, keepdims=True)
    i2 = jnp.min(jnp.where(is_exp & (lane != i1) & (le2 == t2), lane, big), axis=-1, keepdims=True)
    e2 = jnp.exp(t2 - t1)
    w1 = p_grp / (1.0 + e2)
    w2 = w1 * e2
    gate_ref[...] = jnp.where(lane == i1, w1, jnp.where(lane == i2, w2, 0.0))


def _merge(oa, ob, oc, p, x, wb, wo, nf, wr, br, tm, precise):
    r = x.shape[0]
    rows = lambda i: (i, 0)
    const2 = lambda i: (0, 0)
    return pl.pallas_call(
        functools.partial(_merge_kernel, precise=precise),
        grid=(r // tm,),
        in_specs=[
            pl.BlockSpec((tm, BRANCH_W), rows),
            pl.BlockSpec((tm, BRANCH_W), rows),
            pl.BlockSpec((tm, BRANCH_W), rows),
            pl.BlockSpec((tm, N_BRANCH * D_MODEL), rows),
            pl.BlockSpec((tm, D_MODEL), rows),
            pl.BlockSpec((N_BRANCH, BRANCH_W, D_MODEL), lambda i: (0, 0, 0)),
            pl.BlockSpec((D_MODEL, D_MODEL), const2),
            pl.BlockSpec((1, D_MODEL), const2),
            pl.BlockSpec((D_MODEL, LANES), const2),
            pl.BlockSpec((1, LANES), const2),
        ],
        out_specs=[
            pl.BlockSpec((tm, D_MODEL), rows),
            pl.BlockSpec((tm, D_MODEL), rows),
            pl.BlockSpec((tm, LANES), rows),
        ],
        out_shape=[jax.ShapeDtypeStruct((r, D_MODEL), F32), jax.ShapeDtypeStruct((r, D_MODEL), BF16),
                   jax.ShapeDtypeStruct((r, LANES), F32)],
        compiler_params=_params(1),
        name="merge_split" if precise else "merge",
    )(oa, ob, oc, p, x, wb, wo, nf, wr, br)


def _moe_kernel(xn_ref, gate_ref, x1_ref, w1_ref, w3_ref, w2_ref, nfin_ref, *out_refs, final):
    x2_ref = out_refs[0]
    e = pl.program_id(1)

    @pl.when(e == 0)
    def _():
        x2_ref[...] = x1_ref[...]

    xn = xn_ref[...]
    lane = lax.broadcasted_iota(jnp.int32, gate_ref.shape, 1)
    g_e = jnp.sum(jnp.where(lane == LANE_EXPERT + e, gate_ref[...], 0.0), axis=-1, keepdims=True)
    h = _silu(_dot(xn, w1_ref[0])) * _dot(xn, w3_ref[0])
    x2_ref[...] += _dot(h * g_e, w2_ref[0])

    if final:
        @pl.when(e == pl.num_programs(1) - 1)
        def _():
            out_refs[1][...] = _rms(x2_ref[...], nfin_ref[...])


def _moe(xn, gate, x1, w1, w3, w2, nfin, tm, final):
    r = x1.shape[0]
    rows = lambda i, e: (i, 0)
    out_specs = [pl.BlockSpec((tm, D_MODEL), rows)]
    out_shape = [jax.ShapeDtypeStruct((r, D_MODEL), F32)]
    if final:
        out_specs.append(pl.BlockSpec((tm, D_MODEL), rows))
        out_shape.append(jax.ShapeDtypeStruct((r, D_MODEL), F32))
    return pl.pallas_call(
        functools.partial(_moe_kernel, final=final),
        grid=(r // tm, N_EXPERTS),
        in_specs=[
            pl.BlockSpec((tm, D_MODEL), rows),
            pl.BlockSpec((tm, LANES), rows),
            pl.BlockSpec((tm, D_MODEL), rows),
            pl.BlockSpec((1, D_MODEL, D_FF), lambda i, e: (e, 0, 0)),
            pl.BlockSpec((1, D_MODEL, D_FF), lambda i, e: (e, 0, 0)),
            pl.BlockSpec((1, D_FF, D_MODEL), lambda i, e: (e, 0, 0)),
            pl.BlockSpec((1, D_MODEL), lambda i, e: (0, 0)),
        ],
        out_specs=out_specs,
        out_shape=out_shape,
        compiler_params=_params(2),
        name="moe",
    )(xn, gate, x1, w1, w3, w2, nfin)


def _rope_tables(pos):
    half = D_HEAD // 2
    inv = jnp.asarray((ROPE_BASE ** (-np.arange(half, dtype=np.float64) / half)).astype(np.float32))
    ang = pos.astype(F32)[:, None] * inv[None, :]
    cos, sin = jnp.cos(ang), jnp.sin(ang)
    return jnp.concatenate([cos, cos], axis=-1), jnp.concatenate([-sin, sin], axis=-1)


def _retention_tables(t_eff):
    idx = jnp.arange(CHUNK, dtype=F32)
    log_gamma = jnp.log(1.0 - jnp.exp2(-5.0 - jnp.arange(N_HEADS, dtype=F32)))
    incl = idx[:, None] >= idx[None, :]
    rel = jnp.where(incl, idx[:, None] - idx[None, :], 0.0)
    dmat = jnp.where(incl, jnp.exp(rel[None] * log_gamma[:, None, None]), 0.0)
    cross = jnp.exp((idx + 1.0)[:, None] * log_gamma[None, :])
    kdec = jnp.exp((t_eff - 1.0 - idx)[:, None] * log_gamma[None, :])
    tot = jnp.broadcast_to(jnp.exp(t_eff * log_gamma)[None, :], (CHUNK, N_HEADS))
    rtab = jnp.concatenate([cross, kdec, tot, jnp.zeros((CHUNK, LANES - 3 * N_HEADS), F32)], axis=1)
    return dmat, rtab


def _lanes_row(pairs):
    row = jnp.zeros((LANES,), F32)
    for start, vals in pairs:
        row = row.at[start:start + vals.shape[0]].set(vals.astype(F32))
    return row


def _pick(n, candidates):
    for c in candidates:
        if n % c == 0:
            return c
    raise ValueError(f"no tile in {candidates} divides {n}")


def kernel(x_prompt, x_sample, cache_k, cache_v, cache_logf, page_table, state_delta, state_conv, state_ret,
           meta_tokens, norm_mix, norm_ffn, norm_final, w_in, conv_w, a_log, dt_bias, a_norm, b_fbias, c_norm,
           w_branch, w_out, w_router_group, b_router_group, w_router_expert, b_router_expert, w1, w3, w2):
    nb, seq, _ = x_prompt.shape
    ns, ts, _ = x_sample.shape
    depth, n_pool = cache_k.shape[:2]
    n_pages = page_table.shape[1]
    assert seq % CHUNK == 0 and ts <= SAMPLE_ROWS and cache_k.shape[2] == CHUNK
    lp = CHUNK + seq
    n_chunks = lp // CHUNK
    rp = nb * lp
    rs = ns * SAMPLE_ROWS
    tm_p = _pick(rp, (512, 256, 128))
    tm_moe = _pick(rp, (768, 512, 256, 128))
    tm_s = _pick(rs, (256, 128, 64, 32, 16, 8))

    xp = jnp.concatenate([jnp.zeros((nb, PAD_FRONT, D_MODEL), F32),
                          jnp.broadcast_to(meta_tokens[None], (nb, N_META, D_MODEL)), x_prompt],
                         axis=1).reshape(rp, D_MODEL)
    xs = jnp.concatenate([x_sample, jnp.zeros((ns, SAMPLE_ROWS - ts, D_MODEL), F32)], axis=1).reshape(rs, D_MODEL)

    cos_p, sin_p = _rope_tables(jnp.arange(lp) - PAD_FRONT)
    cos_s, sin_s = _rope_tables(n_pages * CHUNK + jnp.arange(CHUNK))
    dmat_p, rtab_p = _retention_tables(float(CHUNK))
    dmat_s, rtab_s = _retention_tables(float(ts))

    assert (ts * N_HEADS) % 8 == 0
    n_cache = depth * n_pool
    cache_k2 = cache_k.reshape(n_cache * CHUNK * N_HEADS, D_HEAD)
    cache_v2 = cache_v.reshape(n_cache * CHUNK * N_HEADS, D_HEAD)
    c_page, tot_page = _page_cumsum(cache_logf.reshape(n_cache, CHUNK * N_HEADS))
    c_page = c_page.reshape(n_cache, 1, CHUNK * N_HEADS)
    tot_page = tot_page.reshape(n_cache, 1, CHUNK * N_HEADS)
    tb = _pick(lp, (384, 256, 128))
    par_p = _pick(nb, (PAR_PROMPT, 1))
    par_s = _pick(ns, (PAR_SAMPLE, 2, 1))
    ppg = _pick(n_pages, (32, 16, 8, 4, 2, 1))

    def sample_heads(a):
        return a.reshape(ns, SAMPLE_ROWS, N_HEADS, D_HEAD)[:, :ts].reshape(ns, ts * N_HEADS, D_HEAD)

    def prompt_rows(a):
        return a.reshape(nb, lp, a.shape[-1])[:, PAD_FRONT:]

    def sample_rows(a):
        return a.reshape(ns, SAMPLE_ROWS, a.shape[-1])[:, :ts]

    def col(a, c, w=BRANCH_W):
        return a[:, c * BRANCH_W:c * BRANCH_W + w]

    outs_p = [[] for _ in range(6)]
    outs_s = [[] for _ in range(6)]
    yp = ys = None
    for l in range(depth):
        final = l == depth - 1
        hp = not final
        wdt = F32 if hp else BF16
        w_main, w_small = _reorder_w_in(w_in, l, wdt)
        g_mix = norm_mix[l][None]
        p_p, ps_p = _proj(xp, g_mix, w_main, w_small, tm_p if hp else tm_moe, 512, hp, F32 if hp else BF16)
        p_s, ps_s = _proj(xs, g_mix, w_main, w_small, tm_s, 512, hp)

        avec = jnp.zeros((8, LANES), F32)
        avec = avec.at[0].set(_lanes_row([(LANE_ALPHA, a_log[l])]))
        avec = avec.at[1].set(_lanes_row([(LANE_ALPHA, dt_bias[l])]))
        avec = avec.at[2].set(_lanes_row([(LANE_F, b_fbias[l])]))
        shared = (conv_w[l], avec, a_norm[l][None], c_norm[l][None])
        oa_p, oc_p, lf_p, cc_p, sd_p, sr_p = _seq(
            p_p, ps_p, cos_p, sin_p, *shared, dmat_p, rtab_p, n_seq=nb, n_chunks=n_chunks, tc_in=CHUNK,
            valid_hi=CHUNK, front_pad=PAD_FRONT, precise=hp, n_par=par_p)
        cb0 = jnp.pad(state_conv[l], ((0, 0), (8 - (CONV_W - 1), 0), (0, 0)))
        oa_s, oc_s, lf_s, cc_s, sd_s, sr_s = _seq(
            p_s, ps_s, cos_s, sin_s, *shared, dmat_s, rtab_s, n_seq=ns, n_chunks=1, tc_in=SAMPLE_ROWS,
            valid_hi=ts, front_pad=0, precise=hp, n_par=par_s, init=(state_delta[l], state_ret[l], cb0))

        ct = jnp.pad(jnp.swapaxes(cc_p[:, LANE_F:LANE_F + N_HEADS].reshape(nb, lp, N_HEADS), 1, 2),
                     ((0, 0), (0, 8 - N_HEADS), (0, 0)))
        ob_p = _fox_prompt(p_p, ct, nb, lp, tb, hp)
        cn = cc_s[:, LANE_F:LANE_F + N_HEADS].reshape(ns, SAMPLE_ROWS, N_HEADS)[:, :ts].reshape(ns, 1, ts * N_HEADS)
        cn = jnp.pad(cn, ((0, 0), (0, 0), (0, CHUNK - ts * N_HEADS)))
        o_s = _fox_sample(page_table, sample_heads(col(p_s, COL_BQ)), sample_heads(col(p_s, COL_BK)),
                          sample_heads(col(p_s, COL_BV)), cn, cache_k2, cache_v2, c_page, tot_page,
                          layer=l, n_pool=n_pool, ppg=ppg, precise=hp)
        ob_s = jnp.pad(o_s.reshape(ns, ts, BRANCH_W), ((0, 0), (0, SAMPLE_ROWS - ts), (0, 0))).reshape(rs, BRANCH_W)

        wr = jnp.concatenate([w_router_group[l], w_router_expert[l],
                              jnp.zeros((D_MODEL, LANES - N_GROUPS - N_EXPERTS), F32)], axis=1)
        br = _lanes_row([(LANE_GROUP, b_router_group[l]), (LANE_EXPERT, b_router_expert[l].reshape(-1))])[None]
        nf = norm_ffn[l][None]
        wb, wo = w_branch[l].astype(wdt), w_out[l].astype(wdt)
        x1_p, xn_p, gate_p = _merge(oa_p, ob_p, oc_p, p_p, xp, wb, wo, nf, wr, br, min(tm_p, 256 if hp else 512), hp)
        x1_s, xn_s, gate_s = _merge(oa_s, ob_s, oc_s, p_s, xs, wb, wo, nf, wr, br, tm_s, hp)

        ew1 = w1[l].reshape(N_EXPERTS, D_MODEL, D_FF).astype(BF16)
        ew3 = w3[l].reshape(N_EXPERTS, D_MODEL, D_FF).astype(BF16)
        ew2 = w2[l].reshape(N_EXPERTS, D_FF, D_MODEL).astype(BF16)
        res_p = _moe(xn_p, gate_p, x1_p, ew1, ew3, ew2, norm_final[None], tm_moe, final)
        res_s = _moe(xn_s, gate_s, x1_s, ew1, ew3, ew2, norm_final[None], tm_s, final)
        xp, xs = res_p[0], res_s[0]
        if final:
            yp, ys = res_p[1], res_s[1]

        for lst, rows_of, p, lf_a, sd, sr in ((outs_p, prompt_rows, p_p, lf_p, sd_p, sr_p),
                                              (outs_s, sample_rows, p_s, lf_s, sd_s, sr_s)):
            k_new = rows_of(col(p, COL_BK)).astype(F32)
            v_new = rows_of(col(p, COL_BV)).astype(F32)
            lst[0].append(k_new.reshape(k_new.shape[:2] + (N_HEADS, D_HEAD)))
            lst[1].append(v_new.reshape(v_new.shape[:2] + (N_HEADS, D_HEAD)))
            lst[2].append(rows_of(lf_a[:, LANE_F:LANE_F + N_HEADS]))
            lst[3].append(sd)
            lst[4].append(rows_of(col(p, COL_AQKV, A_QKV))[:, -(CONV_W - 1):].astype(F32))
            lst[5].append(sr)

    y_prompt = yp.reshape(nb, lp, D_MODEL)[:, CHUNK:]
    y_sample = ys.reshape(ns, SAMPLE_ROWS, D_MODEL)[:, :ts]
    return (y_prompt, y_sample, *(jnp.stack(a) for a in outs_p), *(jnp.stack(a) for a in outs_s))
```
